```python
import jax, jax.numpy as jnp
from jax import lax
import numpy as np

D_MODEL = 1024
BATCH = 8
SEQ = 8192
DEPTH = 4

CHUNK = 64
MEM_LEN = 256
D_CONV = 3 * D_MODEL // 8
CONV_K = 31
D_POOL = 3 * D_MODEL // 8
POOL_WINDOWS = (2, 4, 8, 16)
N_POOL_GROUPS = len(POOL_WINDOWS)
POOL_GROUP = D_POOL // N_POOL_GROUPS
D_MEM = D_MODEL - D_CONV - D_POOL
N_MEM_HEADS = 4
MEM_HEAD_DIM = D_MEM // N_MEM_HEADS
D_MIX = D_CONV + D_POOL + D_MEM
D_IN = 2 * D_CONV + D_POOL + D_MEM
D_FF = 11 * D_MODEL // 4
FFN_K = 3
LN_EPS = 1e-5
DEEPNORM_ALPHA = (2.0 * DEPTH) ** 0.25
DEEPNORM_BETA = (8.0 * DEPTH) ** -0.25

kernel_name = "hybrid_conv_pool_memxattn_deepnorm_encoder"


def layer_norm(x, g, b):
    xf = x.astype(jnp.float32)
    mu = jnp.mean(xf, axis=-1, keepdims=True)
    var = jnp.mean(jnp.square(xf - mu), axis=-1, keepdims=True)
    return ((xf - mu) * lax.rsqrt(var + LN_EPS) * g.astype(jnp.float32) + b.astype(jnp.float32)).astype(x.dtype)


def causal_depthwise_conv(u, w, b):
    k, c = w.shape
    y = lax.conv_general_dilated(
        u, w[:, None, :].astype(u.dtype), window_strides=(1,), padding=[(k - 1, 0)],
        dimension_numbers=("NWC", "WIO", "NWC"), feature_group_count=c)
    return y + b.astype(u.dtype)


def pool_mixer(u, w_pool, scale):
    s = u.shape[1]
    uf = u.astype(jnp.float32)
    c = jnp.cumsum(uf, axis=1)
    maxw = max(POOL_WINDOWS)
    cp = jnp.pad(c, ((0, 0), (maxw, 0), (0, 0)))
    t1 = jnp.arange(1, s + 1, dtype=jnp.float32)[:, None]
    outs = []
    for g, w in enumerate(POOL_WINDOWS):
        lo, hi = g * POOL_GROUP, (g + 1) * POOL_GROUP
        win_sum = c[:, :, lo:hi] - cp[:, maxw - w:maxw - w + s, lo:hi]
        outs.append(win_sum / jnp.minimum(t1, float(w)) - uf[:, :, lo:hi])
    d = jnp.stack(outs, axis=2).astype(u.dtype)
    y = jnp.einsum("bsgc,gcd->bsgd", d, w_pool).reshape(u.shape)
    return y * scale


def memory_cross_attention(q, mem, w_mk, w_mv):
    b, s, _ = q.shape
    m = mem.shape[1]
    qh = q.reshape(b, s, N_MEM_HEADS, MEM_HEAD_DIM)
    kh = (mem @ w_mk).reshape(b, m, N_MEM_HEADS, MEM_HEAD_DIM)
    vh = (mem @ w_mv).reshape(b, m, N_MEM_HEADS, MEM_HEAD_DIM)
    scores = jnp.einsum("bshd,bmhd->bhsm", qh, kh).astype(jnp.float32) * (MEM_HEAD_DIM ** -0.5)
    probs = jax.nn.softmax(scores, axis=-1).astype(vh.dtype)
    o = jnp.einsum("bhsm,bmhd->bshd", probs, vh)
    return o.reshape(b, s, D_MEM)


def mixer(x, mem, w_in, conv_w, conv_b, conv_ln_g, conv_ln_b, pool_w, pool_scale, w_mk, w_mv, w_out):
    h = x @ w_in
    a_val = h[..., :D_CONV]
    a_gate = h[..., D_CONV:2 * D_CONV]
    p_in = h[..., 2 * D_CONV:2 * D_CONV + D_POOL]
    q = h[..., 2 * D_CONV + D_POOL:]
    a = a_val * jax.nn.sigmoid(a_gate)
    a = causal_depthwise_conv(a, conv_w, conv_b)
    a = jax.nn.silu(layer_norm(a, conv_ln_g, conv_ln_b))
    p = pool_mixer(p_in, pool_w, pool_scale)
    mo = memory_cross_attention(q, mem, w_mk, w_mv)
    return jnp.concatenate([a, p, mo], axis=-1) @ w_out


def conv_ffn(x, w_up, ffn_conv_w, ffn_conv_b, w_down):
    u = causal_depthwise_conv(x @ w_up, ffn_conv_w, ffn_conv_b)
    g = u[..., :D_FF]
    v = u[..., D_FF:]
    return (jax.nn.silu(g) * v) @ w_down


def setup_inputs(seed: int = 0) -> dict:
    key = jax.random.key(seed)
    ks = jax.random.split(key, 24)
    f32 = jnp.float32
    nrm = lambda k, shape, s: jax.random.normal(k, shape, f32) * s
    L = DEPTH
    return {
        "x": nrm(ks[0], (BATCH, SEQ, D_MODEL), 1.0),
        "mem": nrm(ks[1], (BATCH, MEM_LEN, D_MODEL), 1.0),
        "ln0_g": 1.0 + nrm(ks[2], (D_MODEL,), 0.02),
        "ln0_b": nrm(ks[3], (D_MODEL,), 0.02),
        "w_in": nrm(ks[4], (L, D_MODEL, D_IN), D_MODEL ** -0.5),
        "conv_w": nrm(ks[5], (L, CONV_K, D_CONV), CONV_K ** -0.5),
        "conv_b": nrm(ks[6], (L, D_CONV), 0.02),
        "conv_ln_g": 1.0 + nrm(ks[7], (L, D_CONV), 0.02),
        "conv_ln_b": nrm(ks[8], (L, D_CONV), 0.02),
        "pool_w": nrm(ks[9], (L, N_POOL_GROUPS, POOL_GROUP, POOL_GROUP), POOL_GROUP ** -0.5),
        "pool_scale": 1.0 + nrm(ks[10], (L, D_POOL), 0.1),
        "w_mk": nrm(ks[11], (L, D_MODEL, D_MEM), D_MODEL ** -0.5),
        "w_mv": nrm(ks[12], (L, D_MODEL, D_MEM), D_MODEL ** -0.5 * DEEPNORM_BETA),
        "w_out": nrm(ks[13], (L, D_MIX, D_MODEL), D_MIX ** -0.5 * DEEPNORM_BETA),
        "ln1_g": 1.0 + nrm(ks[14], (L, D_MODEL), 0.02),
        "ln1_b": nrm(ks[15], (L, D_MODEL), 0.02),
        "w_up": nrm(ks[16], (L, D_MODEL, 2 * D_FF), D_MODEL ** -0.5),
        "ffn_conv_w": nrm(ks[17], (L, FFN_K, 2 * D_FF), FFN_K ** -0.5),
        "ffn_conv_b": nrm(ks[18], (L, 2 * D_FF), 0.02),
        "w_down": nrm(ks[19], (L, D_FF, D_MODEL), D_FF ** -0.5 * DEEPNORM_BETA),
        "ln2_g": 1.0 + nrm(ks[20], (L, D_MODEL), 0.02),
        "ln2_b": nrm(ks[21], (L, D_MODEL), 0.02),
    }


def reference(x, mem, ln0_g, ln0_b, w_in, conv_w, conv_b, conv_ln_g, conv_ln_b, pool_w, pool_scale,
              w_mk, w_mv, w_out, ln1_g, ln1_b, w_up, ffn_conv_w, ffn_conv_b, w_down, ln2_g, ln2_b):
    x = layer_norm(x, ln0_g, ln0_b)
    for l in range(DEPTH):
        y = mixer(x, mem, w_in[l], conv_w[l], conv_b[l], conv_ln_g[l], conv_ln_b[l],
                  pool_w[l], pool_scale[l], w_mk[l], w_mv[l], w_out[l])
        x = layer_norm(DEEPNORM_ALPHA * x + y, ln1_g[l], ln1_b[l])
        y = conv_ffn(x, w_up[l], ffn_conv_w[l], ffn_conv_b[l], w_down[l])
        x = layer_norm(DEEPNORM_ALPHA * x + y, ln2_g[l], ln2_b[l])
    return x
```

```python
import functools

import jax
import jax.numpy as jnp
from jax import lax
from jax.experimental import pallas as pl
from jax.experimental.pallas import tpu as pltpu

D_MODEL = 1024
DEPTH = 4
MEM_LEN = 256
D_CONV = 384
CONV_K = 31
D_POOL = 384
POOL_WINDOWS = (2, 4, 8, 16)
POOL_GROUP = D_POOL // len(POOL_WINDOWS)
D_MEM = 256
N_MEM_HEADS = 4
MEM_HEAD_DIM = D_MEM // N_MEM_HEADS
D_IN = 2 * D_CONV + D_POOL + D_MEM
D_FF = 2816
FFN_K = 3
LN_EPS = 1e-5
DEEPNORM_ALPHA = (2.0 * DEPTH) ** 0.25

SUBLANES = 8
LANES = 128

ROW_TILE = 256
CONV_HALO = 32
POOL_HALO = 16
ROW_CHUNK = 32
FF_CHUNK = 512
VMEM_LIMIT_BYTES = 56 * 1024 * 1024

F32 = jnp.float32
BF16 = jnp.bfloat16


def _layer_norm(z, g, b):
    mu = jnp.mean(z, axis=-1, keepdims=True)
    zc = z - mu
    var = jnp.mean(zc * zc, axis=-1, keepdims=True)
    return zc * lax.rsqrt(var + LN_EPS) * g + b


def _ln_kernel(x_ref, g_ref, b_ref, o_ref):
    o_ref[0] = _layer_norm(x_ref[0], g_ref[...], b_ref[...])


def _mixer_kernel(x_ref, mem_ref, w_in_ref, conv_w_ref, conv_b_ref, cln_g_ref, cln_b_ref,
                  w_pool_ref, pool_scale_ref, w_mkt_ref, w_mv_ref, w_out_ref, ln_g_ref, ln_b_ref,
                  o_ref, abuf, pbuf, dbuf, kbd, vbd, cat, *, tq):
    i = pl.program_id(1)

    @pl.when(i == 0)
    def _start_of_sequence():
        abuf[0:CONV_HALO, :] = jnp.zeros((CONV_HALO, D_CONV), F32)
        pbuf[0:POOL_HALO, :] = jnp.zeros((POOL_HALO, D_POOL), F32)
        memb = mem_ref[0].astype(BF16)
        kt = lax.dot_general(w_mkt_ref[...], memb, (((1,), (1,)), ((), ())),
                             preferred_element_type=F32) * (MEM_HEAD_DIM ** -0.5)
        v = jnp.dot(memb, w_mv_ref[...], preferred_element_type=F32)
        kshape = (D_MEM, N_MEM_HEADS * MEM_LEN)
        k_row_head = lax.broadcasted_iota(jnp.int32, kshape, 0) // MEM_HEAD_DIM
        k_col_head = lax.broadcasted_iota(jnp.int32, kshape, 1) // MEM_LEN
        kt4 = jnp.concatenate([kt] * N_MEM_HEADS, axis=1)
        kbd[...] = jnp.where(k_row_head == k_col_head, kt4, 0.0).astype(BF16)
        vshape = (N_MEM_HEADS * MEM_LEN, D_MEM)
        v_row_head = lax.broadcasted_iota(jnp.int32, vshape, 0) // MEM_LEN
        v_col_head = lax.broadcasted_iota(jnp.int32, vshape, 1) // MEM_HEAD_DIM
        v4 = jnp.concatenate([v] * N_MEM_HEADS, axis=0)
        same_head = v_row_head == v_col_head
        vbd[:, 0:D_MEM] = jnp.where(same_head, v4, 0.0).astype(BF16)
        vbd[:, D_MEM:2 * D_MEM] = jnp.where(same_head, 1.0, 0.0).astype(BF16)

    x = x_ref[0]
    h = jnp.dot(x.astype(BF16), w_in_ref[...], preferred_element_type=F32)
    a = h[:, 0:D_CONV] * jax.nn.sigmoid(h[:, D_CONV:2 * D_CONV])
    abuf[CONV_HALO:CONV_HALO + tq, :] = a
    pbuf[POOL_HALO:POOL_HALO + tq, :] = h[:, 2 * D_CONV:2 * D_CONV + D_POOL]
    q = h[:, 2 * D_CONV + D_POOL:D_IN].astype(BF16)

    for r0 in range(0, tq, ROW_CHUNK):
        base = r0 + (CONV_HALO - (CONV_K - 1))
        acc = jnp.zeros((ROW_CHUNK, D_CONV), F32)
        for j in range(CONV_K):
            acc = acc + conv_w_ref[j:j + 1, :] * abuf[base + j:base + j + ROW_CHUNK, :]
        acc = acc + conv_b_ref[...]
        y = _layer_norm(acc, cln_g_ref[...], cln_b_ref[...])
        y = y * jax.nn.sigmoid(y)
        cat[r0:r0 + ROW_CHUNK, 0:D_CONV] = y.astype(BF16)

    lane = lax.broadcasted_iota(jnp.int32, (ROW_CHUNK, D_POOL), 1)
    row = lax.broadcasted_iota(jnp.int32, (ROW_CHUNK, D_POOL), 0)
    group = lane // POOL_GROUP
    window = jnp.where(group == 0, POOL_WINDOWS[0],
                       jnp.where(group == 1, POOL_WINDOWS[1],
                                 jnp.where(group == 2, POOL_WINDOWS[2], POOL_WINDOWS[3])))

    for r0 in range(0, tq, ROW_CHUNK):
        cur = pbuf[r0 + POOL_HALO:r0 + POOL_HALO + ROW_CHUNK, :]
        sums = []
        s = cur
        k = 1
        for w in POOL_WINDOWS:
            while k < w:
                s = s + pbuf[r0 + POOL_HALO - k:r0 + POOL_HALO - k + ROW_CHUNK, :]
                k += 1
            sums.append(s)
        win = jnp.where(group == 0, sums[0],
                        jnp.where(group == 1, sums[1], jnp.where(group == 2, sums[2], sums[3])))
        t1 = i * tq + r0 + row + 1
        count = jnp.minimum(t1, window).astype(F32)
        dbuf[r0:r0 + ROW_CHUNK, :] = (win / count - cur).astype(BF16)
    pooled = jnp.dot(dbuf[...], w_pool_ref[...], preferred_element_type=F32) * pool_scale_ref[...]
    cat[:, D_CONV:D_CONV + D_POOL] = pooled.astype(BF16)

    abuf[0:CONV_HALO, :] = abuf[tq:tq + CONV_HALO, :]
    pbuf[0:POOL_HALO, :] = pbuf[tq:tq + POOL_HALO, :]

    s = jnp.dot(q, kbd[...], preferred_element_type=F32)
    es = []
    for hd in range(N_MEM_HEADS):
        sh = s[:, hd * MEM_LEN:(hd + 1) * MEM_LEN]
        es.append(jnp.exp(sh - jnp.max(sh, axis=-1, keepdims=True)))
    e = jnp.concatenate(es, axis=1).astype(BF16)
    ov = jnp.dot(e, vbd[...], preferred_element_type=F32)
    cat[:, D_CONV + D_POOL:] = (ov[:, 0:D_MEM] / ov[:, D_MEM:2 * D_MEM]).astype(BF16)

    y = jnp.dot(cat[...], w_out_ref[...], preferred_element_type=F32)
    o_ref[0] = _layer_norm(DEEPNORM_ALPHA * x + y, ln_g_ref[...], ln_b_ref[...])


def _ff_chunks():
    chunks = []
    c0 = 0
    while c0 < D_FF:
        cn = min(FF_CHUNK, D_FF - c0)
        chunks.append((c0, cn))
        c0 += cn
    return chunks


def _ffn_kernel(x_ref, w_up_ref, conv_w_ref, conv_b_ref, w_down_ref, ln_g_ref, ln_b_ref,
                o_ref, ubuf, act, *, tq):
    i = pl.program_id(1)
    look_back = FFN_K - 1

    @pl.when(i == 0)
    def _start_of_sequence():
        ubuf[0:SUBLANES, :] = jnp.zeros((SUBLANES, 2 * D_FF), F32)

    x = x_ref[0]
    xb = x.astype(BF16)

    def conv_part(col, cn):
        u = jnp.dot(xb, w_up_ref[:, col:col + cn], preferred_element_type=F32)
        ubuf[SUBLANES:SUBLANES + tq, col:col + cn] = u
        y = conv_w_ref[look_back:look_back + 1, col:col + cn] * u
        for k in range(look_back):
            shift = look_back - k
            y = y + conv_w_ref[k:k + 1, col:col + cn] * ubuf[SUBLANES - shift:SUBLANES - shift + tq, col:col + cn]
        return y + conv_b_ref[:, col:col + cn]

    for c0, cn in _ff_chunks():
        g = conv_part(c0, cn)
        v = conv_part(D_FF + c0, cn)
        act[:, c0:c0 + cn] = (g * jax.nn.sigmoid(g) * v).astype(BF16)

    ubuf[0:SUBLANES, :] = ubuf[tq:tq + SUBLANES, :]

    y = jnp.dot(act[...], w_down_ref[...], preferred_element_type=F32)
    o_ref[0] = _layer_norm(DEEPNORM_ALPHA * x + y, ln_g_ref[...], ln_b_ref[...])


def _whole(shape):
    return pl.BlockSpec(shape, lambda b, i: (0,) * len(shape), pipeline_mode=pl.Buffered(1))


def _row(v):
    return v.reshape(1, -1)


def _params():
    return pltpu.CompilerParams(dimension_semantics=("arbitrary", "arbitrary"),
                                vmem_limit_bytes=VMEM_LIMIT_BYTES)


def _ln_call(x, g, b, tq):
    bsz, seq, d = x.shape
    tile = pl.BlockSpec((1, tq, d), lambda b_, i: (b_, i, 0))
    return pl.pallas_call(
        _ln_kernel,
        grid=(bsz, seq // tq),
        in_specs=[tile, _whole((1, d)), _whole((1, d))],
        out_specs=tile,
        out_shape=jax.ShapeDtypeStruct(x.shape, x.dtype),
        compiler_params=_params(),
        name="ln0",
    )(x, _row(g), _row(b))


def _mixer_call(x, mem, w_in, conv_w, conv_b, cln_g, cln_b, w_pool_bd, pool_scale, w_mkt, w_mv, w_out,
                ln_g, ln_b, tq):
    bsz, seq, d = x.shape
    tile = pl.BlockSpec((1, tq, d), lambda b_, i: (b_, i, 0))
    mem_spec = pl.BlockSpec((1, MEM_LEN, d), lambda b_, i: (b_, 0, 0))
    args = (x, mem, w_in, conv_w, _row(conv_b), _row(cln_g), _row(cln_b), w_pool_bd, _row(pool_scale),
            w_mkt, w_mv, w_out, _row(ln_g), _row(ln_b))
    in_specs = [tile, mem_spec] + [_whole(a.shape) for a in args[2:]]
    return pl.pallas_call(
        functools.partial(_mixer_kernel, tq=tq),
        grid=(bsz, seq // tq),
        in_specs=in_specs,
        out_specs=tile,
        out_shape=jax.ShapeDtypeStruct(x.shape, x.dtype),
        scratch_shapes=[
            pltpu.VMEM((tq + CONV_HALO, D_CONV), F32),
            pltpu.VMEM((tq + POOL_HALO, D_POOL), F32),
            pltpu.VMEM((tq, D_POOL), BF16),
            pltpu.VMEM((D_MEM, N_MEM_HEADS * MEM_LEN), BF16),
            pltpu.VMEM((N_MEM_HEADS * MEM_LEN, 2 * D_MEM), BF16),
            pltpu.VMEM((tq, D_MODEL), BF16),
        ],
        compiler_params=_params(),
        name="mixer",
    )(*args)


def _ffn_call(x, w_up, conv_w, conv_b, w_down, ln_g, ln_b, tq):
    bsz, seq, d = x.shape
    tile = pl.BlockSpec((1, tq, d), lambda b_, i: (b_, i, 0))
    args = (x, w_up, conv_w, _row(conv_b), w_down, _row(ln_g), _row(ln_b))
    in_specs = [tile] + [_whole(a.shape) for a in args[1:]]
    return pl.pallas_call(
        functools.partial(_ffn_kernel, tq=tq),
        grid=(bsz, seq // tq),
        in_specs=in_specs,
        out_specs=tile,
        out_shape=jax.ShapeDtypeStruct(x.shape, x.dtype),
        scratch_shapes=[
            pltpu.VMEM((tq + SUBLANES, 2 * D_FF), F32),
            pltpu.VMEM((tq, D_FF), BF16),
        ],
        compiler_params=_params(),
        name="ffn",
    )(*args)


def _block_diag(w):
    g, n, _ = w.shape
    eye = jnp.eye(g, dtype=w.dtype)
    return jnp.einsum("gh,gcd->gchd", eye, w).reshape(g * n, g * n)


def kernel(x, mem, ln0_g, ln0_b, w_in, conv_w, conv_b, conv_ln_g, conv_ln_b, pool_w, pool_scale, w_mk, w_mv, w_out, ln1_g, ln1_b, w_up, ffn_conv_w, ffn_conv_b, w_down, ln2_g, ln2_b):
    bsz, seq, d = x.shape
    assert d == D_MODEL and mem.shape == (bsz, MEM_LEN, D_MODEL)
    tq = min(ROW_TILE, seq)
    assert seq % tq == 0 and tq % ROW_CHUNK == 0 and tq >= CONV_HALO
    x = _ln_call(x, ln0_g, ln0_b, tq)
    for l in range(DEPTH):
        x = _mixer_call(
            x, mem, w_in[l].astype(BF16), conv_w[l], conv_b[l], conv_ln_g[l], conv_ln_b[l],
            _block_diag(pool_w[l]).astype(BF16), pool_scale[l], w_mk[l].T.astype(BF16), w_mv[l].astype(BF16),
            w_out[l].astype(BF16), ln1_g[l], ln1_b[l], tq)
        x = _ffn_call(x, w_up[l].astype(BF16), ffn_conv_w[l], ffn_conv_b[l], w_down[l].astype(BF16),
                      ln2_g[l], ln2_b[l], tq)
    return x
```

```python
import functools

import jax
import jax.numpy as jnp
from jax import lax
from jax.experimental import pallas as pl
from jax.experimental.pallas import tpu as pltpu

D_MODEL = 1024
DEPTH = 4
MEM_LEN = 256
D_CONV = 384
CONV_K = 31
D_POOL = 384
POOL_WINDOWS = (2, 4, 8, 16)
POOL_GROUP = D_POOL // len(POOL_WINDOWS)
D_MEM = 256
N_MEM_HEADS = 4
MEM_HEAD_DIM = D_MEM // N_MEM_HEADS
D_IN = 2 * D_CONV + D_POOL + D_MEM
D_FF = 2816
FFN_K = 3
LN_EPS = 1e-5
DEEPNORM_ALPHA = (2.0 * DEPTH) ** 0.25

SUBLANES = 8
LANES = 128

ROW_TILE = 256
CONV_BACK = CONV_K - 1
POOL_BACK = max(POOL_WINDOWS) - 1
FFN_BACK = FFN_K - 1
ROW_CHUNK = 32
FF_CHUNK = 512
VMEM_LIMIT_BYTES = 56 * 1024 * 1024

F32 = jnp.float32
BF16 = jnp.bfloat16


def _layer_norm(z, g, b):
    mu = jnp.mean(z, axis=-1, keepdims=True)
    zc = z - mu
    var = jnp.mean(zc * zc, axis=-1, keepdims=True)
    return zc * lax.rsqrt(var + LN_EPS) * g + b


def _to_segment_major(x, tq):
    return jnp.swapaxes(x.reshape(SUBLANES, tq // SUBLANES, x.shape[-1]), 0, 1).reshape(tq, x.shape[-1])


def _to_frame_order(x, tq):
    return jnp.swapaxes(x.reshape(tq // SUBLANES, SUBLANES, x.shape[-1]), 0, 1).reshape(tq, x.shape[-1])


def _halo(cur_tail, prev_tail, mbuf):
    rows, cols = cur_tail.shape
    zeros = jnp.zeros((SUBLANES, cols), F32)
    sub = lax.broadcasted_iota(jnp.int32, (rows + SUBLANES, cols), 0) % SUBLANES
    mbuf[...] = jnp.where(sub == SUBLANES - 1,
                          jnp.concatenate([prev_tail, zeros], axis=0),
                          jnp.concatenate([zeros, cur_tail], axis=0))
    return mbuf[SUBLANES - 1:SUBLANES - 1 + rows, :]


def _mixer_kernel(x_ref, mem_ref, ln0_g_ref, ln0_b_ref, w_in_ref, conv_w_ref, conv_b_ref, cln_g_ref, cln_b_ref,
                  w_pool_ref, pool_scale_ref, pool_count_ref, w_mkt_ref, w_mv_ref, w_out_ref, ln_g_ref, ln_b_ref,
                  o_ref, ebuf, atail, ptail, amix, pmix, dbuf, kbd, vbd, cat, *, tq, first_layer):
    i = pl.program_id(1)
    conv_halo = SUBLANES * CONV_BACK
    pool_halo = SUBLANES * POOL_BACK

    @pl.when(i == 0)
    def _start_of_sequence():
        atail[...] = jnp.zeros(atail.shape, F32)
        ptail[...] = jnp.zeros(ptail.shape, F32)
        memb = mem_ref[0].astype(BF16)
        kt = lax.dot_general(w_mkt_ref[...], memb, (((1,), (1,)), ((), ())),
                             preferred_element_type=F32) * (MEM_HEAD_DIM ** -0.5)
        v = jnp.dot(memb, w_mv_ref[...], preferred_element_type=F32)
        kshape = (D_MEM, N_MEM_HEADS * MEM_LEN)
        k_row_head = lax.broadcasted_iota(jnp.int32, kshape, 0) // MEM_HEAD_DIM
        k_col_head = lax.broadcasted_iota(jnp.int32, kshape, 1) // MEM_LEN
        kt4 = jnp.concatenate([kt] * N_MEM_HEADS, axis=1)
        kbd[...] = jnp.where(k_row_head == k_col_head, kt4, 0.0).astype(BF16)
        vshape = (N_MEM_HEADS * MEM_LEN, D_MEM)
        v_row_head = lax.broadcasted_iota(jnp.int32, vshape, 0) // MEM_LEN
        v_col_head = lax.broadcasted_iota(jnp.int32, vshape, 1) // MEM_HEAD_DIM
        v4 = jnp.concatenate([v] * N_MEM_HEADS, axis=0)
        same_head = v_row_head == v_col_head
        vbd[:, 0:D_MEM] = jnp.where(same_head, v4, 0.0).astype(BF16)
        vbd[:, D_MEM:2 * D_MEM] = jnp.where(same_head, 1.0, 0.0).astype(BF16)

    x = x_ref[0]
    if first_layer:
        x = _to_segment_major(_layer_norm(x, ln0_g_ref[...], ln0_b_ref[...]), tq)
    h = jnp.dot(x.astype(BF16), w_in_ref[...], preferred_element_type=F32)
    a = h[:, 0:D_CONV] * jax.nn.sigmoid(h[:, D_CONV:2 * D_CONV])
    p_in = h[:, 2 * D_CONV:2 * D_CONV + D_POOL]
    q = h[:, 2 * D_CONV + D_POOL:D_IN].astype(BF16)

    a_tail = a[tq - conv_halo:tq, :]
    ebuf[0:conv_halo, :] = _halo(a_tail, atail[...], amix)
    ebuf[conv_halo:conv_halo + tq, :] = a
    atail[...] = a_tail
    for r0 in range(0, tq, ROW_CHUNK):
        acc = jnp.zeros((ROW_CHUNK, D_CONV), F32)
        for j in range(CONV_K):
            acc = acc + conv_w_ref[j:j + 1, :] * ebuf[r0 + SUBLANES * j:r0 + SUBLANES * j + ROW_CHUNK, :]
        acc = acc + conv_b_ref[...]
        y = _layer_norm(acc, cln_g_ref[...], cln_b_ref[...])
        y = y * jax.nn.sigmoid(y)
        cat[r0:r0 + ROW_CHUNK, 0:D_CONV] = y.astype(BF16)

    p_tail = p_in[tq - pool_halo:tq, :]
    ext = jnp.concatenate([_halo(p_tail, ptail[...], pmix), p_in], axis=0)
    ptail[...] = p_tail
    sums = []
    s = ext
    span = 1
    for w in POOL_WINDOWS:
        while span < w:
            step = SUBLANES * span
            s = s[step:, :] + s[:-step, :]
            span *= 2
        sums.append(s[s.shape[0] - tq:, :])
    group = lax.broadcasted_iota(jnp.int32, (tq, D_POOL), 1) // POOL_GROUP
    win = jnp.where(group == 0, sums[0], jnp.where(group == 1, sums[1], jnp.where(group == 2, sums[2], sums[3])))
    dbuf[...] = (win / pool_count_ref[0] - p_in).astype(BF16)
    pooled = jnp.dot(dbuf[...], w_pool_ref[...], preferred_element_type=F32) * pool_scale_ref[...]
    cat[:, D_CONV:D_CONV + D_POOL] = pooled.astype(BF16)

    s = jnp.dot(q, kbd[...], preferred_element_type=F32)
    es = []
    for hd in range(N_MEM_HEADS):
        sh = s[:, hd * MEM_LEN:(hd + 1) * MEM_LEN]
        es.append(jnp.exp(sh - jnp.max(sh, axis=-1, keepdims=True)))
    e = jnp.concatenate(es, axis=1).astype(BF16)
    ov = jnp.dot(e, vbd[...], preferred_element_type=F32)
    cat[:, D_CONV + D_POOL:] = (ov[:, 0:D_MEM] / ov[:, D_MEM:2 * D_MEM]).astype(BF16)

    y = jnp.dot(cat[...], w_out_ref[...], preferred_element_type=F32)
    o_ref[0] = _layer_norm(DEEPNORM_ALPHA * x + y, ln_g_ref[...], ln_b_ref[...])


def _ff_chunks():
    chunks = []
    c0 = 0
    while c0 < D_FF:
        cn = min(FF_CHUNK, D_FF - c0)
        chunks.append((c0, cn))
        c0 += cn
    return chunks


def _ffn_kernel(x_ref, w_up_ref, conv_w_ref, conv_b_ref, w_down_ref, ln_g_ref, ln_b_ref,
                o_ref, utail, umix, act, *, tq, last_layer):
    i = pl.program_id(1)
    halo = SUBLANES * FFN_BACK

    @pl.when(i == 0)
    def _start_of_sequence():
        utail[...] = jnp.zeros(utail.shape, F32)

    x = x_ref[0]
    xb = x.astype(BF16)

    def conv_part(col, cn):
        cols = slice(col, col + cn)
        u = jnp.dot(xb, w_up_ref[:, cols], preferred_element_type=F32)
        u_tail = u[tq - halo:tq, :]
        ext = jnp.concatenate([_halo(u_tail, utail[:, cols], umix.at[:, cols]), u], axis=0)
        utail[:, cols] = u_tail
        y = conv_b_ref[:, cols] + conv_w_ref[FFN_BACK:FFN_K, cols] * u
        for k in range(FFN_BACK):
            y = y + conv_w_ref[k:k + 1, cols] * ext[SUBLANES * k:SUBLANES * k + tq, :]
        return y

    for c0, cn in _ff_chunks():
        g = conv_part(c0, cn)
        v = conv_part(D_FF + c0, cn)
        act[:, c0:c0 + cn] = (g * jax.nn.sigmoid(g) * v).astype(BF16)

    y = jnp.dot(act[...], w_down_ref[...], preferred_element_type=F32)
    out = _layer_norm(DEEPNORM_ALPHA * x + y, ln_g_ref[...], ln_b_ref[...])
    o_ref[0] = _to_frame_order(out, tq) if last_layer else out


def _whole(shape):
    return pl.BlockSpec(shape, lambda b, i: (0,) * len(shape), pipeline_mode=pl.Buffered(1))


def _row(v):
    return v.reshape(1, -1)


def _params():
    return pltpu.CompilerParams(dimension_semantics=("arbitrary", "arbitrary"),
                                vmem_limit_bytes=VMEM_LIMIT_BYTES)


def _pool_counts(tq):
    rows = jnp.arange(tq)
    frame = (rows % SUBLANES) * (tq // SUBLANES) + rows // SUBLANES
    window = jnp.repeat(jnp.asarray(POOL_WINDOWS, jnp.int32), POOL_GROUP)
    first = jnp.minimum(frame[:, None] + 1, window[None, :])
    later = jnp.broadcast_to(window[None, :], (tq, D_POOL))
    return jnp.stack([first, later]).astype(F32)


def _mixer_call(x, mem, ln0_g, ln0_b, w_in, conv_w, conv_b, cln_g, cln_b, w_pool_bd, pool_scale, w_mkt, w_mv, w_out,
                ln_g, ln_b, tq, first_layer):
    bsz, seq, d = x.shape
    tile = pl.BlockSpec((1, tq, d), lambda b_, i: (b_, i, 0))
    mem_spec = pl.BlockSpec((1, MEM_LEN, d), lambda b_, i: (b_, 0, 0))
    count_spec = pl.BlockSpec((1, tq, D_POOL), lambda b_, i: (jnp.minimum(i, 1), 0, 0))
    args = [x, mem, _row(ln0_g), _row(ln0_b), w_in, conv_w, _row(conv_b), _row(cln_g), _row(cln_b), w_pool_bd,
            _row(pool_scale), _pool_counts(tq), w_mkt, w_mv, w_out, _row(ln_g), _row(ln_b)]
    in_specs = [tile, mem_spec] + [count_spec if n == 11 else _whole(a.shape) for n, a in enumerate(args) if n >= 2]
    return pl.pallas_call(
        functools.partial(_mixer_kernel, tq=tq, first_layer=first_layer),
        grid=(bsz, seq // tq),
        in_specs=in_specs,
        out_specs=tile,
        out_shape=jax.ShapeDtypeStruct(x.shape, x.dtype),
        scratch_shapes=[
            pltpu.VMEM((SUBLANES * CONV_BACK + tq, D_CONV), F32),
            pltpu.VMEM((SUBLANES * CONV_BACK, D_CONV), F32),
            pltpu.VMEM((SUBLANES * POOL_BACK, D_POOL), F32),
            pltpu.VMEM((SUBLANES * (CONV_BACK + 1), D_CONV), F32),
            pltpu.VMEM((SUBLANES * (POOL_BACK + 1), D_POOL), F32),
            pltpu.VMEM((tq, D_POOL), BF16),
            pltpu.VMEM((D_MEM, N_MEM_HEADS * MEM_LEN), BF16),
            pltpu.VMEM((N_MEM_HEADS * MEM_LEN, 2 * D_MEM), BF16),
            pltpu.VMEM((tq, D_MODEL), BF16),
        ],
        compiler_params=_params(),
        name="mixer",
    )(*args)


def _ffn_call(x, w_up, conv_w, conv_b, w_down, ln_g, ln_b, tq, last_layer):
    bsz, seq, d = x.shape
    tile = pl.BlockSpec((1, tq, d), lambda b_, i: (b_, i, 0))
    args = (x, w_up, conv_w, _row(conv_b), w_down, _row(ln_g), _row(ln_b))
    in_specs = [tile] + [_whole(a.shape) for a in args[1:]]
    return pl.pallas_call(
        functools.partial(_ffn_kernel, tq=tq, last_layer=last_layer),
        grid=(bsz, seq // tq),
        in_specs=in_specs,
        out_specs=tile,
        out_shape=jax.ShapeDtypeStruct(x.shape, x.dtype),
        scratch_shapes=[
            pltpu.VMEM((SUBLANES * FFN_BACK, 2 * D_FF), F32),
            pltpu.VMEM((SUBLANES * (FFN_BACK + 1), 2 * D_FF), F32),
            pltpu.VMEM((tq, D_FF), BF16),
        ],
        compiler_params=_params(),
        name="ffn",
    )(*args)


def _block_diag(w):
    g, n, _ = w.shape
    eye = jnp.eye(g, dtype=w.dtype)
    return jnp.einsum("gh,gcd->gchd", eye, w).reshape(g * n, g * n)


def kernel(x, mem, ln0_g, ln0_b, w_in, conv_w, conv_b, conv_ln_g, conv_ln_b, pool_w, pool_scale, w_mk, w_mv, w_out, ln1_g, ln1_b, w_up, ffn_conv_w, ffn_conv_b, w_down, ln2_g, ln2_b):
    bsz, seq, d = x.shape
    assert d == D_MODEL and mem.shape == (bsz, MEM_LEN, D_MODEL)
    tq = min(ROW_TILE, seq)
    assert seq % tq == 0 and tq % ROW_CHUNK == 0 and tq // SUBLANES > CONV_BACK
    for l in range(DEPTH):
        x = _mixer_call(
            x, mem, ln0_g, ln0_b, w_in[l].astype(BF16), conv_w[l], conv_b[l], conv_ln_g[l], conv_ln_b[l],
            _block_diag(pool_w[l]).astype(BF16), pool_scale[l], w_mk[l].T.astype(BF16), w_mv[l].astype(BF16),
            w_out[l].astype(BF16), ln1_g[l], ln1_b[l], tq, first_layer=(l == 0))
        x = _ffn_call(x, w_up[l].astype(BF16), ffn_conv_w[l], ffn_conv_b[l], w_down[l].astype(BF16),
                      ln2_g[l], ln2_b[l], tq, last_layer=(l == DEPTH - 1))
    return x
```

```python
import functools

import jax
import jax.numpy as jnp
from jax import lax
from jax.experimental import pallas as pl
from jax.experimental.pallas import tpu as pltpu

D_MODEL = 1024
DEPTH = 4
MEM_LEN = 256
D_CONV = 384
CONV_K = 31
D_POOL = 384
POOL_WINDOWS = (2, 4, 8, 16)
POOL_GROUP = D_POOL // len(POOL_WINDOWS)
D_MEM = 256
N_MEM_HEADS = 4
MEM_HEAD_DIM = D_MEM // N_MEM_HEADS
D_IN = 2 * D_CONV + D_POOL + D_MEM
D_FF = 2816
FFN_K = 3
LN_EPS = 1e-5
DEEPNORM_ALPHA = (2.0 * DEPTH) ** 0.25

SUBLANES = 8
LANES = 128

ROW_TILE = 256
TILES_PER_STEP = 2
CONV_BACK = CONV_K - 1
POOL_BACK = max(POOL_WINDOWS) - 1
FFN_BACK = FFN_K - 1
ROW_CHUNK = 32
FF_CHUNK = 512
VMEM_LIMIT_BYTES = 56 * 1024 * 1024

F32 = jnp.float32
BF16 = jnp.bfloat16


def _layer_norm(z, g, b):
    mu = jnp.mean(z, axis=-1, keepdims=True)
    zc = z - mu
    var = jnp.mean(zc * zc, axis=-1, keepdims=True)
    return zc * lax.rsqrt(var + LN_EPS) * g + b


def _to_segment_major(x, tq):
    return jnp.swapaxes(x.reshape(SUBLANES, tq // SUBLANES, x.shape[-1]), 0, 1).reshape(tq, x.shape[-1])


def _to_frame_order(x, tq):
    return jnp.swapaxes(x.reshape(tq // SUBLANES, SUBLANES, x.shape[-1]), 0, 1).reshape(tq, x.shape[-1])


def _halo(cur_tail, prev_tail, mbuf):
    rows, cols = cur_tail.shape
    zeros = jnp.zeros((SUBLANES, cols), F32)
    sub = lax.broadcasted_iota(jnp.int32, (rows + SUBLANES, cols), 0) % SUBLANES
    mbuf[...] = jnp.where(sub == SUBLANES - 1,
                          jnp.concatenate([prev_tail, zeros], axis=0),
                          jnp.concatenate([zeros, cur_tail], axis=0))
    return mbuf[SUBLANES - 1:SUBLANES - 1 + rows, :]


def _mixer_kernel(x_ref, mem_ref, ln0_g_ref, ln0_b_ref, w_in_ref, conv_w_ref, conv_b_ref, cln_g_ref, cln_b_ref,
                  w_pool_ref, pool_scale_ref, count_first_ref, count_later_ref, w_mkt_ref, w_mv_ref, w_out_ref,
                  ln_g_ref, ln_b_ref, o_ref, ebuf, pbuf, qbuf, atail, ptail, amix, pmix, kbd, vbd, cat, *, tq, first_layer):
    i = pl.program_id(1)
    conv_halo = SUBLANES * CONV_BACK
    pool_halo = SUBLANES * POOL_BACK

    @pl.when(i == 0)
    def _start_of_sequence():
        atail[...] = jnp.zeros(atail.shape, F32)
        ptail[...] = jnp.zeros(ptail.shape, F32)
        memb = mem_ref[0].astype(BF16)
        kt = lax.dot_general(w_mkt_ref[...], memb, (((1,), (1,)), ((), ())),
                             preferred_element_type=F32) * (MEM_HEAD_DIM ** -0.5)
        v = jnp.dot(memb, w_mv_ref[...], preferred_element_type=F32)
        kshape = (D_MEM, N_MEM_HEADS * MEM_LEN)
        k_row_head = lax.broadcasted_iota(jnp.int32, kshape, 0) // MEM_HEAD_DIM
        k_col_head = lax.broadcasted_iota(jnp.int32, kshape, 1) // MEM_LEN
        kt4 = jnp.concatenate([kt] * N_MEM_HEADS, axis=1)
        kbd[...] = jnp.where(k_row_head == k_col_head, kt4, 0.0).astype(BF16)
        vshape = (N_MEM_HEADS * MEM_LEN, D_MEM)
        v_row_head = lax.broadcasted_iota(jnp.int32, vshape, 0) // MEM_LEN
        v_col_head = lax.broadcasted_iota(jnp.int32, vshape, 1) // MEM_HEAD_DIM
        v4 = jnp.concatenate([v] * N_MEM_HEADS, axis=0)
        same_head = v_row_head == v_col_head
        vbd[:, 0:D_MEM] = jnp.where(same_head, v4, 0.0).astype(BF16)
        vbd[:, D_MEM:2 * D_MEM] = jnp.where(same_head, 1.0, 0.0).astype(BF16)

    group = lax.broadcasted_iota(jnp.int32, (tq, D_POOL), 1) // POOL_GROUP
    st = [dict() for _ in range(TILES_PER_STEP)]

    def rows_of(t):
        return slice(t * tq, (t + 1) * tq)

    def front_x(t):
        x = x_ref[0, rows_of(t), :]
        if first_layer:
            x = _to_segment_major(_layer_norm(x, ln0_g_ref[...], ln0_b_ref[...]), tq)
        st[t].update(x=x, xb=x.astype(BF16))

    def front_a(t):
        h = jnp.dot(st[t]["xb"], w_in_ref[:, 0:2 * D_CONV], preferred_element_type=F32)
        a = h[:, 0:D_CONV] * jax.nn.sigmoid(h[:, D_CONV:2 * D_CONV])
        a_prev = atail[...] if t == 0 else st[t - 1]["a_tail"]
        a_tail = a[tq - conv_halo:tq, :]
        ebuf[t, 0:conv_halo, :] = _halo(a_tail, a_prev, amix.at[t])
        ebuf[t, conv_halo:conv_halo + tq, :] = a
        st[t].update(a_tail=a_tail)

    def front_p(t):
        p_in = jnp.dot(st[t]["xb"], w_in_ref[:, 2 * D_CONV:2 * D_CONV + D_POOL], preferred_element_type=F32)
        p_prev = ptail[...] if t == 0 else st[t - 1]["p_tail"]
        p_tail = p_in[tq - pool_halo:tq, :]
        pbuf[t, 0:pool_halo, :] = _halo(p_tail, p_prev, pmix.at[t])
        pbuf[t, pool_halo:pool_halo + tq, :] = p_in
        st[t].update(p_tail=p_tail)

    def front_q(t):
        q = jnp.dot(st[t]["xb"], w_in_ref[:, 2 * D_CONV + D_POOL:D_IN], preferred_element_type=F32)
        qbuf[t] = q.astype(BF16)

    def conv(t, r0):
        accs = [jnp.zeros((SUBLANES, D_CONV), F32)] * (ROW_CHUNK // SUBLANES)
        for j in range(CONV_K):
            wj = conv_w_ref[SUBLANES * j:SUBLANES * (j + 1), :]
            for k in range(len(accs)):
                lo = r0 + SUBLANES * (j + k)
                accs[k] = accs[k] + wj * ebuf[t, lo:lo + SUBLANES, :]
        acc = jnp.concatenate(accs, axis=0) + conv_b_ref[...]
        y = _layer_norm(acc, cln_g_ref[...], cln_b_ref[...])
        y = y * jax.nn.sigmoid(y)
        cat[t, r0:r0 + ROW_CHUNK, 0:D_CONV] = y.astype(BF16)

    def pool(t):
        s = pbuf[t]
        p_in = s[pool_halo:, :]
        sums = []
        span = 1
        for w in POOL_WINDOWS:
            while span < w:
                step = SUBLANES * span
                s = s[step:, :] + s[:-step, :]
                span *= 2
            sums.append(s[s.shape[0] - tq:, :])
        win = jnp.where(group == 0, sums[0], jnp.where(group == 1, sums[1], jnp.where(group == 2, sums[2], sums[3])))
        count = count_first_ref[0] if t == 0 else count_later_ref[0]
        d = (win / count - p_in).astype(BF16)
        pooled = jnp.dot(d, w_pool_ref[...], preferred_element_type=F32) * pool_scale_ref[...]
        cat[t, :, D_CONV:D_CONV + D_POOL] = pooled.astype(BF16)

    def attn(t):
        s = jnp.dot(qbuf[t], kbd[...], preferred_element_type=F32)
        es = []
        for hd in range(N_MEM_HEADS):
            sh = s[:, hd * MEM_LEN:(hd + 1) * MEM_LEN]
            es.append(jnp.exp(sh - jnp.max(sh, axis=-1, keepdims=True)))
        e = jnp.concatenate(es, axis=1).astype(BF16)
        ov = jnp.dot(e, vbd[...], preferred_element_type=F32)
        cat[t, :, D_CONV + D_POOL:] = (ov[:, 0:D_MEM] / ov[:, D_MEM:2 * D_MEM]).astype(BF16)

    def out(t):
        y = jnp.dot(cat[t], w_out_ref[...], preferred_element_type=F32)
        o_ref[0, rows_of(t), :] = _layer_norm(DEEPNORM_ALPHA * st[t]["x"] + y, ln_g_ref[...], ln_b_ref[...])

    chunks = list(range(0, tq, ROW_CHUNK))
    n = len(chunks)

    def convs(t, lo, hi):
        for r0 in chunks[n * lo // 8:n * hi // 8]:
            conv(t, r0)

    front_x(0); front_a(0); front_p(0); front_q(0)
    front_x(1)
    convs(0, 0, 1); front_a(1)
    convs(0, 1, 3); front_p(1)
    convs(0, 3, 4); front_q(1)
    convs(0, 4, 5); pool(0)
    convs(0, 5, 6); attn(0)
    convs(0, 6, 8)
    convs(1, 0, 2); out(0)
    convs(1, 2, 8)
    pool(1); attn(1); out(1)

    atail[...] = st[TILES_PER_STEP - 1]["a_tail"]
    ptail[...] = st[TILES_PER_STEP - 1]["p_tail"]


def _ff_chunks():
    chunks = []
    c0 = 0
    while c0 < D_FF:
        cn = min(FF_CHUNK, D_FF - c0)
        chunks.append((c0, cn))
        c0 += cn
    return chunks


def _ffn_kernel(x_ref, w_up_ref, conv_w_ref, conv_b_ref, w_down_ref, ln_g_ref, ln_b_ref,
                o_ref, utail, umix, act, *, tq, last_layer):
    i = pl.program_id(1)
    halo = SUBLANES * FFN_BACK

    @pl.when(i == 0)
    def _start_of_sequence():
        utail[...] = jnp.zeros(utail.shape, F32)

    u_prev = {}
    for t in range(TILES_PER_STEP):
        rows = slice(t * tq, (t + 1) * tq)
        x = x_ref[0, rows, :]
        xb = x.astype(BF16)

        def conv_part(col, cn):
            cols = slice(col, col + cn)
            u = jnp.dot(xb, w_up_ref[:, cols], preferred_element_type=F32)
            u_tail = u[tq - halo:tq, :]
            prev = utail[:, cols] if t == 0 else u_prev[col]
            ext = jnp.concatenate([_halo(u_tail, prev, umix.at[t, :, cols]), u], axis=0)
            u_prev[col] = u_tail
            y = conv_b_ref[:, cols] + conv_w_ref[FFN_BACK:FFN_K, cols] * u
            for k in range(FFN_BACK):
                y = y + conv_w_ref[k:k + 1, cols] * ext[SUBLANES * k:SUBLANES * k + tq, :]
            return y

        for c0, cn in _ff_chunks():
            g = conv_part(c0, cn)
            v = conv_part(D_FF + c0, cn)
            act[t, :, c0:c0 + cn] = (g * jax.nn.sigmoid(g) * v).astype(BF16)

        y = jnp.dot(act[t], w_down_ref[...], preferred_element_type=F32)
        out = _layer_norm(DEEPNORM_ALPHA * x + y, ln_g_ref[...], ln_b_ref[...])
        o_ref[0, rows, :] = _to_frame_order(out, tq) if last_layer else out

    for col, u_tail in u_prev.items():
        utail[:, col:col + u_tail.shape[1]] = u_tail


def _whole(shape):
    return pl.BlockSpec(shape, lambda b, i: (0,) * len(shape), pipeline_mode=pl.Buffered(1))


def _row(v):
    return v.reshape(1, -1)


def _params():
    return pltpu.CompilerParams(dimension_semantics=("arbitrary", "arbitrary"),
                                vmem_limit_bytes=VMEM_LIMIT_BYTES)


def _pool_counts(tq):
    rows = jnp.arange(tq)
    frame = (rows % SUBLANES) * (tq // SUBLANES) + rows // SUBLANES
    window = jnp.repeat(jnp.asarray(POOL_WINDOWS, jnp.int32), POOL_GROUP)
    first = jnp.minimum(frame[:, None] + 1, window[None, :])
    later = jnp.broadcast_to(window[None, :], (tq, D_POOL))
    return jnp.stack([first, later]).astype(F32)


def _mixer_call(x, mem, ln0_g, ln0_b, w_in, conv_w, conv_b, cln_g, cln_b, w_pool_bd, pool_scale, w_mkt, w_mv, w_out,
                ln_g, ln_b, tq, first_layer):
    bsz, seq, d = x.shape
    step_rows = TILES_PER_STEP * tq
    tile = pl.BlockSpec((1, step_rows, d), lambda b_, i: (b_, i, 0))
    mem_spec = pl.BlockSpec((1, MEM_LEN, d), lambda b_, i: (b_, 0, 0))
    count_first = pl.BlockSpec((1, tq, D_POOL), lambda b_, i: (jnp.minimum(i, 1), 0, 0))
    count_later = pl.BlockSpec((1, tq, D_POOL), lambda b_, i: (1, 0, 0), pipeline_mode=pl.Buffered(1))
    counts = _pool_counts(tq)
    conv_w_rows = jnp.repeat(conv_w, SUBLANES, axis=0)
    args = [x, mem, _row(ln0_g), _row(ln0_b), w_in, conv_w_rows, _row(conv_b), _row(cln_g), _row(cln_b), w_pool_bd,
            _row(pool_scale), counts, counts, w_mkt, w_mv, w_out, _row(ln_g), _row(ln_b)]
    in_specs = [tile, mem_spec] + [_whole(a.shape) for a in args[2:]]
    in_specs[11], in_specs[12] = count_first, count_later
    return pl.pallas_call(
        functools.partial(_mixer_kernel, tq=tq, first_layer=first_layer),
        grid=(bsz, seq // step_rows),
        in_specs=in_specs,
        out_specs=tile,
        out_shape=jax.ShapeDtypeStruct(x.shape, x.dtype),
        scratch_shapes=[
            pltpu.VMEM((TILES_PER_STEP, SUBLANES * CONV_BACK + tq, D_CONV), F32),
            pltpu.VMEM((TILES_PER_STEP, SUBLANES * POOL_BACK + tq, D_POOL), F32),
            pltpu.VMEM((TILES_PER_STEP, tq, D_MEM), BF16),
            pltpu.VMEM((SUBLANES * CONV_BACK, D_CONV), F32),
            pltpu.VMEM((SUBLANES * POOL_BACK, D_POOL), F32),
            pltpu.VMEM((TILES_PER_STEP, SUBLANES * (CONV_BACK + 1), D_CONV), F32),
            pltpu.VMEM((TILES_PER_STEP, SUBLANES * (POOL_BACK + 1), D_POOL), F32),
            pltpu.VMEM((D_MEM, N_MEM_HEADS * MEM_LEN), BF16),
            pltpu.VMEM((N_MEM_HEADS * MEM_LEN, 2 * D_MEM), BF16),
            pltpu.VMEM((TILES_PER_STEP, tq, D_MODEL), BF16),
        ],
        compiler_params=_params(),
        name="mixer",
    )(*args)


def _ffn_call(x, w_up, conv_w, conv_b, w_down, ln_g, ln_b, tq, last_layer):
    bsz, seq, d = x.shape
    step_rows = TILES_PER_STEP * tq
    tile = pl.BlockSpec((1, step_rows, d), lambda b_, i: (b_, i, 0))
    args = (x, w_up, conv_w, _row(conv_b), w_down, _row(ln_g), _row(ln_b))
    in_specs = [tile] + [_whole(a.shape) for a in args[1:]]
    return pl.pallas_call(
        functools.partial(_ffn_kernel, tq=tq, last_layer=last_layer),
        grid=(bsz, seq // step_rows),
        in_specs=in_specs,
        out_specs=tile,
        out_shape=jax.ShapeDtypeStruct(x.shape, x.dtype),
        scratch_shapes=[
            pltpu.VMEM((SUBLANES * FFN_BACK, 2 * D_FF), F32),
            pltpu.VMEM((TILES_PER_STEP, SUBLANES * (FFN_BACK + 1), 2 * D_FF), F32),
            pltpu.VMEM((TILES_PER_STEP, tq, D_FF), BF16),
        ],
        compiler_params=_params(),
        name="ffn",
    )(*args)


def _block_diag(w):
    g, n, _ = w.shape
    eye = jnp.eye(g, dtype=w.dtype)
    return jnp.einsum("gh,gcd->gchd", eye, w).reshape(g * n, g * n)


def kernel(x, mem, ln0_g, ln0_b, w_in, conv_w, conv_b, conv_ln_g, conv_ln_b, pool_w, pool_scale, w_mk, w_mv, w_out, ln1_g, ln1_b, w_up, ffn_conv_w, ffn_conv_b, w_down, ln2_g, ln2_b):
    bsz, seq, d = x.shape
    assert d == D_MODEL and mem.shape == (bsz, MEM_LEN, D_MODEL)
    tq = ROW_TILE
    assert seq % (TILES_PER_STEP * tq) == 0 and tq % ROW_CHUNK == 0 and tq // SUBLANES > CONV_BACK
    for l in range(DEPTH):
        x = _mixer_call(
            x, mem, ln0_g, ln0_b, w_in[l].astype(BF16), conv_w[l], conv_b[l], conv_ln_g[l], conv_ln_b[l],
            _block_diag(pool_w[l]).astype(BF16), pool_scale[l], w_mk[l].T.astype(BF16), w_mv[l].astype(BF16),
            w_out[l].astype(BF16), ln1_g[l], ln1_b[l], tq, first_layer=(l == 0))
        x = _ffn_call(x, w_up[l].astype(BF16), ffn_conv_w[l], ffn_conv_b[l], w_down[l].astype(BF16),
                      ln2_g[l], ln2_b[l], tq, last_layer=(l == DEPTH - 1))
    return x
```

```python
import functools

import jax
import jax.numpy as jnp
from jax import lax
from jax.experimental import pallas as pl
from jax.experimental.pallas import tpu as pltpu

D_MODEL = 1024
DEPTH = 4
MEM_LEN = 256
D_CONV = 384
CONV_K = 31
D_POOL = 384
POOL_WINDOWS = (2, 4, 8, 16)
POOL_GROUP = D_POOL // len(POOL_WINDOWS)
D_MEM = 256
N_MEM_HEADS = 4
MEM_HEAD_DIM = D_MEM // N_MEM_HEADS
D_IN = 2 * D_CONV + D_POOL + D_MEM
D_FF = 2816
FFN_K = 3
LN_EPS = 1e-5
DEEPNORM_ALPHA = (2.0 * DEPTH) ** 0.25

SUBLANES = 8
LANES = 128

ROW_TILE = 256
TILES_PER_STEP = 4
CONV_BACK = CONV_K - 1
POOL_BACK = max(POOL_WINDOWS) - 1
FFN_BACK = FFN_K - 1
ROW_CHUNK = 32
FF_CHUNK = 512
VMEM_LIMIT_BYTES = 56 * 1024 * 1024

F32 = jnp.float32
BF16 = jnp.bfloat16


def _layer_norm(z, g, b):
    mu = jnp.mean(z, axis=-1, keepdims=True)
    zc = z - mu
    var = jnp.mean(zc * zc, axis=-1, keepdims=True)
    return zc * lax.rsqrt(var + LN_EPS) * g + b


def _to_segment_major(x, tq):
    return jnp.swapaxes(x.reshape(SUBLANES, tq // SUBLANES, x.shape[-1]), 0, 1).reshape(tq, x.shape[-1])


def _to_frame_order(x, tq):
    return jnp.swapaxes(x.reshape(tq // SUBLANES, SUBLANES, x.shape[-1]), 0, 1).reshape(tq, x.shape[-1])


def _halo(cur_tail, prev_tail, mbuf):
    rows, cols = cur_tail.shape
    zeros = jnp.zeros((SUBLANES, cols), F32)
    sub = lax.broadcasted_iota(jnp.int32, (rows + SUBLANES, cols), 0) % SUBLANES
    mbuf[...] = jnp.where(sub == SUBLANES - 1,
                          jnp.concatenate([prev_tail, zeros], axis=0),
                          jnp.concatenate([zeros, cur_tail], axis=0))
    return mbuf[SUBLANES - 1:SUBLANES - 1 + rows, :]


def _mixer_kernel(x_ref, mem_ref, ln0_g_ref, ln0_b_ref, w_in_ref, conv_w_ref, conv_b_ref, cln_g_ref, cln_b_ref,
                  w_pool_ref, pool_scale_ref, count_first_ref, count_later_ref, w_mkt_ref, w_mv_ref, w_out_ref,
                  ln_g_ref, ln_b_ref, o_ref, ebuf, pbuf, qbuf, atail, ptail, amix, pmix, kbd, vbd, cat, *, tq, first_layer):
    i = pl.program_id(1)
    conv_halo = SUBLANES * CONV_BACK
    pool_halo = SUBLANES * POOL_BACK

    @pl.when(i == 0)
    def _start_of_sequence():
        atail[...] = jnp.zeros(atail.shape, F32)
        ptail[...] = jnp.zeros(ptail.shape, F32)
        memb = mem_ref[0].astype(BF16)
        kt = lax.dot_general(w_mkt_ref[...], memb, (((1,), (1,)), ((), ())),
                             preferred_element_type=F32) * (MEM_HEAD_DIM ** -0.5)
        v = jnp.dot(memb, w_mv_ref[...], preferred_element_type=F32)
        kshape = (D_MEM, N_MEM_HEADS * MEM_LEN)
        k_row_head = lax.broadcasted_iota(jnp.int32, kshape, 0) // MEM_HEAD_DIM
        k_col_head = lax.broadcasted_iota(jnp.int32, kshape, 1) // MEM_LEN
        kt4 = jnp.concatenate([kt] * N_MEM_HEADS, axis=1)
        kbd[...] = jnp.where(k_row_head == k_col_head, kt4, 0.0).astype(BF16)
        vshape = (N_MEM_HEADS * MEM_LEN, D_MEM)
        v_row_head = lax.broadcasted_iota(jnp.int32, vshape, 0) // MEM_LEN
        v_col_head = lax.broadcasted_iota(jnp.int32, vshape, 1) // MEM_HEAD_DIM
        v4 = jnp.concatenate([v] * N_MEM_HEADS, axis=0)
        same_head = v_row_head == v_col_head
        vbd[:, 0:D_MEM] = jnp.where(same_head, v4, 0.0).astype(BF16)
        vbd[:, D_MEM:2 * D_MEM] = jnp.where(same_head, 1.0, 0.0).astype(BF16)

    group = lax.broadcasted_iota(jnp.int32, (tq, D_POOL), 1) // POOL_GROUP
    st = [dict() for _ in range(TILES_PER_STEP)]

    def rows_of(t):
        return slice(t * tq, (t + 1) * tq)

    def front_x(t):
        x = x_ref[0, rows_of(t), :]
        if first_layer:
            x = _to_segment_major(_layer_norm(x, ln0_g_ref[...], ln0_b_ref[...]), tq)
        st[t].update(x=x, xb=x.astype(BF16))

    def front_a(t):
        h = jnp.dot(st[t]["xb"], w_in_ref[:, 0:2 * D_CONV], preferred_element_type=F32)
        a = h[:, 0:D_CONV] * jax.nn.sigmoid(h[:, D_CONV:2 * D_CONV])
        a_prev = atail[...] if t == 0 else st[t - 1]["a_tail"]
        a_tail = a[tq - conv_halo:tq, :]
        ebuf[t, 0:conv_halo, :] = _halo(a_tail, a_prev, amix.at[t])
        ebuf[t, conv_halo:conv_halo + tq, :] = a
        st[t].update(a_tail=a_tail)

    def front_p(t):
        p_in = jnp.dot(st[t]["xb"], w_in_ref[:, 2 * D_CONV:2 * D_CONV + D_POOL], preferred_element_type=F32)
        p_prev = ptail[...] if t == 0 else st[t - 1]["p_tail"]
        p_tail = p_in[tq - pool_halo:tq, :]
        pbuf[t, 0:pool_halo, :] = _halo(p_tail, p_prev, pmix.at[t])
        pbuf[t, pool_halo:pool_halo + tq, :] = p_in
        st[t].update(p_tail=p_tail)

    def front_q(t):
        q = jnp.dot(st[t]["xb"], w_in_ref[:, 2 * D_CONV + D_POOL:D_IN], preferred_element_type=F32)
        qbuf[t] = q.astype(BF16)

    def conv(t, r0):
        accs = [jnp.zeros((SUBLANES, D_CONV), F32)] * (ROW_CHUNK // SUBLANES)
        for j in range(CONV_K):
            wj = conv_w_ref[SUBLANES * j:SUBLANES * (j + 1), :]
            for k in range(len(accs)):
                lo = r0 + SUBLANES * (j + k)
                accs[k] = accs[k] + wj * ebuf[t, lo:lo + SUBLANES, :]
        acc = jnp.concatenate(accs, axis=0) + conv_b_ref[...]
        y = _layer_norm(acc, cln_g_ref[...], cln_b_ref[...])
        y = y * jax.nn.sigmoid(y)
        cat[t, r0:r0 + ROW_CHUNK, 0:D_CONV] = y.astype(BF16)

    def pool(t):
        s = pbuf[t]
        p_in = s[pool_halo:, :]
        sums = []
        span = 1
        for w in POOL_WINDOWS:
            while span < w:
                step = SUBLANES * span
                s = s[step:, :] + s[:-step, :]
                span *= 2
            sums.append(s[s.shape[0] - tq:, :])
        win = jnp.where(group == 0, sums[0], jnp.where(group == 1, sums[1], jnp.where(group == 2, sums[2], sums[3])))
        count = count_first_ref[0] if t == 0 else count_later_ref[0]
        d = (win / count - p_in).astype(BF16)
        pooled = jnp.dot(d, w_pool_ref[...], preferred_element_type=F32) * pool_scale_ref[...]
        cat[t, :, D_CONV:D_CONV + D_POOL] = pooled.astype(BF16)

    def attn(t):
        s = jnp.dot(qbuf[t], kbd[...], preferred_element_type=F32)
        es = []
        for hd in range(N_MEM_HEADS):
            sh = s[:, hd * MEM_LEN:(hd + 1) * MEM_LEN]
            es.append(jnp.exp(sh - jnp.max(sh, axis=-1, keepdims=True)))
        e = jnp.concatenate(es, axis=1).astype(BF16)
        ov = jnp.dot(e, vbd[...], preferred_element_type=F32)
        cat[t, :, D_CONV + D_POOL:] = (ov[:, 0:D_MEM] / ov[:, D_MEM:2 * D_MEM]).astype(BF16)

    def out(t):
        y = jnp.dot(cat[t], w_out_ref[...], preferred_element_type=F32)
        o_ref[0, rows_of(t), :] = _layer_norm(DEEPNORM_ALPHA * st[t]["x"] + y, ln_g_ref[...], ln_b_ref[...])

    chunks = list(range(0, tq, ROW_CHUNK))
    n = len(chunks)

    def convs(t, lo, hi):
        for r0 in chunks[n * lo // 8:n * hi // 8]:
            conv(t, r0)

    front_x(0); front_a(0); front_p(0); front_q(0)
    for t in range(TILES_PER_STEP):
        nxt = t + 1 < TILES_PER_STEP
        convs(t, 0, 1)
        if nxt:
            front_x(t + 1); front_a(t + 1)
        convs(t, 1, 2)
        if t > 0:
            out(t - 1)
        convs(t, 2, 3)
        if nxt:
            front_p(t + 1)
        convs(t, 3, 4)
        if nxt:
            front_q(t + 1)
        convs(t, 4, 5); pool(t)
        convs(t, 5, 6); attn(t)
        convs(t, 6, 8)
    out(TILES_PER_STEP - 1)

    atail[...] = st[TILES_PER_STEP - 1]["a_tail"]
    ptail[...] = st[TILES_PER_STEP - 1]["p_tail"]


def _ff_chunks():
    chunks = []
    c0 = 0
    while c0 < D_FF:
        cn = min(FF_CHUNK, D_FF - c0)
        chunks.append((c0, cn))
        c0 += cn
    return chunks


def _ffn_kernel(x_ref, w_up_ref, conv_w_ref, conv_b_ref, w_down_ref, ln_g_ref, ln_b_ref,
                o_ref, utail, umix, act, *, tq, last_layer):
    i = pl.program_id(1)
    halo = SUBLANES * FFN_BACK

    @pl.when(i == 0)
    def _start_of_sequence():
        utail[...] = jnp.zeros(utail.shape, F32)

    u_prev = {}
    for t in range(TILES_PER_STEP):
        rows = slice(t * tq, (t + 1) * tq)
        x = x_ref[0, rows, :]
        xb = x.astype(BF16)

        def conv_part(col, cn):
            cols = slice(col, col + cn)
            u = jnp.dot(xb, w_up_ref[:, cols], preferred_element_type=F32)
            u_tail = u[tq - halo:tq, :]
            prev = utail[:, cols] if t == 0 else u_prev[col]
            ext = jnp.concatenate([_halo(u_tail, prev, umix.at[t, :, cols]), u], axis=0)
            u_prev[col] = u_tail
            y = conv_b_ref[:, cols] + conv_w_ref[FFN_BACK:FFN_K, cols] * u
            for k in range(FFN_BACK):
                y = y + conv_w_ref[k:k + 1, cols] * ext[SUBLANES * k:SUBLANES * k + tq, :]
            return y

        for c0, cn in _ff_chunks():
            g = conv_part(c0, cn)
            v = conv_part(D_FF + c0, cn)
            act[t, :, c0:c0 + cn] = (g * jax.nn.sigmoid(g) * v).astype(BF16)

        y = jnp.dot(act[t], w_down_ref[...], preferred_element_type=F32)
        out = _layer_norm(DEEPNORM_ALPHA * x + y, ln_g_ref[...], ln_b_ref[...])
        o_ref[0, rows, :] = _to_frame_order(out, tq) if last_layer else out

    for col, u_tail in u_prev.items():
        utail[:, col:col + u_tail.shape[1]] = u_tail


def _whole(shape):
    return pl.BlockSpec(shape, lambda b, i: (0,) * len(shape), pipeline_mode=pl.Buffered(1))


def _row(v):
    return v.reshape(1, -1)


def _params():
    return pltpu.CompilerParams(dimension_semantics=("arbitrary", "arbitrary"),
                                vmem_limit_bytes=VMEM_LIMIT_BYTES)


def _pool_counts(tq):
    rows = jnp.arange(tq)
    frame = (rows % SUBLANES) * (tq // SUBLANES) + rows // SUBLANES
    window = jnp.repeat(jnp.asarray(POOL_WINDOWS, jnp.int32), POOL_GROUP)
    first = jnp.minimum(frame[:, None] + 1, window[None, :])
    later = jnp.broadcast_to(window[None, :], (tq, D_POOL))
    return jnp.stack([first, later]).astype(F32)


def _mixer_call(x, mem, ln0_g, ln0_b, w_in, conv_w, conv_b, cln_g, cln_b, w_pool_bd, pool_scale, w_mkt, w_mv, w_out,
                ln_g, ln_b, tq, first_layer):
    bsz, seq, d = x.shape
    step_rows = TILES_PER_STEP * tq
    tile = pl.BlockSpec((1, step_rows, d), lambda b_, i: (b_, i, 0))
    mem_spec = pl.BlockSpec((1, MEM_LEN, d), lambda b_, i: (b_, 0, 0))
    count_first = pl.BlockSpec((1, tq, D_POOL), lambda b_, i: (jnp.minimum(i, 1), 0, 0))
    count_later = pl.BlockSpec((1, tq, D_POOL), lambda b_, i: (1, 0, 0), pipeline_mode=pl.Buffered(1))
    counts = _pool_counts(tq)
    conv_w_rows = jnp.repeat(conv_w, SUBLANES, axis=0)
    args = [x, mem, _row(ln0_g), _row(ln0_b), w_in, conv_w_rows, _row(conv_b), _row(cln_g), _row(cln_b), w_pool_bd,
            _row(pool_scale), counts, counts, w_mkt, w_mv, w_out, _row(ln_g), _row(ln_b)]
    in_specs = [tile, mem_spec] + [_whole(a.shape) for a in args[2:]]
    in_specs[11], in_specs[12] = count_first, count_later
    return pl.pallas_call(
        functools.partial(_mixer_kernel, tq=tq, first_layer=first_layer),
        grid=(bsz, seq // step_rows),
        in_specs=in_specs,
        out_specs=tile,
        out_shape=jax.ShapeDtypeStruct(x.shape, x.dtype),
        scratch_shapes=[
            pltpu.VMEM((TILES_PER_STEP, SUBLANES * CONV_BACK + tq, D_CONV), F32),
            pltpu.VMEM((TILES_PER_STEP, SUBLANES * POOL_BACK + tq, D_POOL), F32),
            pltpu.VMEM((TILES_PER_STEP, tq, D_MEM), BF16),
            pltpu.VMEM((SUBLANES * CONV_BACK, D_CONV), F32),
            pltpu.VMEM((SUBLANES * POOL_BACK, D_POOL), F32),
            pltpu.VMEM((TILES_PER_STEP, SUBLANES * (CONV_BACK + 1), D_CONV), F32),
            pltpu.VMEM((TILES_PER_STEP, SUBLANES * (POOL_BACK + 1), D_POOL), F32),
            pltpu.VMEM((D_MEM, N_MEM_HEADS * MEM_LEN), BF16),
            pltpu.VMEM((N_MEM_HEADS * MEM_LEN, 2 * D_MEM), BF16),
            pltpu.VMEM((TILES_PER_STEP, tq, D_MODEL), BF16),
        ],
        compiler_params=_params(),
        name="mixer",
    )(*args)


def _ffn_call(x, w_up, conv_w, conv_b, w_down, ln_g, ln_b, tq, last_layer):
    bsz, seq, d = x.shape
    step_rows = TILES_PER_STEP * tq
    tile = pl.BlockSpec((1, step_rows, d), lambda b_, i: (b_, i, 0))
    args = (x, w_up, conv_w, _row(conv_b), w_down, _row(ln_g), _row(ln_b))
    in_specs = [tile] + [_whole(a.shape) for a in args[1:]]
    return pl.pallas_call(
        functools.partial(_ffn_kernel, tq=tq, last_layer=last_layer),
        grid=(bsz, seq // step_rows),
        in_specs=in_specs,
        out_specs=tile,
        out_shape=jax.ShapeDtypeStruct(x.shape, x.dtype),
        scratch_shapes=[
            pltpu.VMEM((SUBLANES * FFN_BACK, 2 * D_FF), F32),
            pltpu.VMEM((TILES_PER_STEP, SUBLANES * (FFN_BACK + 1), 2 * D_FF), F32),
            pltpu.VMEM((TILES_PER_STEP, tq, D_FF), BF16),
        ],
        compiler_params=_params(),
        name="ffn",
    )(*args)


def _block_diag(w):
    g, n, _ = w.shape
    eye = jnp.eye(g, dtype=w.dtype)
    return jnp.einsum("gh,gcd->gchd", eye, w).reshape(g * n, g * n)


def kernel(x, mem, ln0_g, ln0_b, w_in, conv_w, conv_b, conv_ln_g, conv_ln_b, pool_w, pool_scale, w_mk, w_mv, w_out, ln1_g, ln1_b, w_up, ffn_conv_w, ffn_conv_b, w_down, ln2_g, ln2_b):
    bsz, seq, d = x.shape
    assert d == D_MODEL and mem.shape == (bsz, MEM_LEN, D_MODEL)
    tq = ROW_TILE
    assert seq % (TILES_PER_STEP * tq) == 0 and tq % ROW_CHUNK == 0 and tq // SUBLANES > CONV_BACK
    for l in range(DEPTH):
        x = _mixer_call(
            x, mem, ln0_g, ln0_b, w_in[l].astype(BF16), conv_w[l], conv_b[l], conv_ln_g[l], conv_ln_b[l],
            _block_diag(pool_w[l]).astype(BF16), pool_scale[l], w_mk[l].T.astype(BF16), w_mv[l].astype(BF16),
            w_out[l].astype(BF16), ln1_g[l], ln1_b[l], tq, first_layer=(l == 0))
        x = _ffn_call(x, w_up[l].astype(BF16), ffn_conv_w[l], ffn_conv_b[l], w_down[l].astype(BF16),
                      ln2_g[l], ln2_b[l], tq, last_layer=(l == DEPTH - 1))
    return x
```

```python
import functools

import jax
import jax.numpy as jnp
from jax import lax
from jax.experimental import pallas as pl
from jax.experimental.pallas import tpu as pltpu

D_MODEL = 1024
DEPTH = 4
MEM_LEN = 256
D_CONV = 384
CONV_K = 31
D_POOL = 384
POOL_WINDOWS = (2, 4, 8, 16)
POOL_GROUP = D_POOL // len(POOL_WINDOWS)
D_MEM = 256
N_MEM_HEADS = 4
MEM_HEAD_DIM = D_MEM // N_MEM_HEADS
D_IN = 2 * D_CONV + D_POOL + D_MEM
D_FF = 2816
FFN_K = 3
LN_EPS = 1e-5
DEEPNORM_ALPHA = (2.0 * DEPTH) ** 0.25

SUBLANES = 8
LANES = 128

ROW_TILE = 256
TILES_PER_STEP = 2
CONV_BACK = CONV_K - 1
POOL_BACK = max(POOL_WINDOWS) - 1
FFN_BACK = FFN_K - 1
ROW_CHUNK = 32
FF_CHUNK = 512
VMEM_LIMIT_BYTES = 56 * 1024 * 1024

F32 = jnp.float32
BF16 = jnp.bfloat16


def _layer_norm(z, g, b):
    mu = jnp.mean(z, axis=-1, keepdims=True)
    zc = z - mu
    var = jnp.mean(zc * zc, axis=-1, keepdims=True)
    return zc * lax.rsqrt(var + LN_EPS) * g + b


def _to_segment_major(x, tq):
    return jnp.swapaxes(x.reshape(SUBLANES, tq // SUBLANES, x.shape[-1]), 0, 1).reshape(tq, x.shape[-1])


def _to_frame_order(x, tq):
    return jnp.swapaxes(x.reshape(tq // SUBLANES, SUBLANES, x.shape[-1]), 0, 1).reshape(tq, x.shape[-1])


def _halo(cur_tail, prev_tail, mbuf):
    rows, cols = cur_tail.shape
    zeros = jnp.zeros((SUBLANES, cols), F32)
    sub = lax.broadcasted_iota(jnp.int32, (rows + SUBLANES, cols), 0) % SUBLANES
    mbuf[...] = jnp.where(sub == SUBLANES - 1,
                          jnp.concatenate([prev_tail, zeros], axis=0),
                          jnp.concatenate([zeros, cur_tail], axis=0))
    return mbuf[SUBLANES - 1:SUBLANES - 1 + rows, :]


def _mixer_kernel(x_ref, mem_ref, ln0_g_ref, ln0_b_ref, w_in_ref, conv_w_ref, conv_b_ref, cln_g_ref, cln_b_ref,
                  w_pool_ref, pool_scale_ref, count_first_ref, count_later_ref, w_mkt_ref, w_mv_ref, w_out_ref,
                  ln_g_ref, ln_b_ref, o_ref, ebuf, pbuf, qbuf, atail, ptail, amix, pmix, kbd, vbd, cat,
                  *, tq, first_layer):
    i = pl.program_id(1)
    conv_halo = SUBLANES * CONV_BACK
    pool_halo = SUBLANES * POOL_BACK

    @pl.when(i == 0)
    def _start_of_sequence():
        atail[...] = jnp.zeros(atail.shape, F32)
        ptail[...] = jnp.zeros(ptail.shape, F32)
        memb = mem_ref[0].astype(BF16)
        kt = lax.dot_general(w_mkt_ref[...], memb, (((1,), (1,)), ((), ())),
                             preferred_element_type=F32) * (MEM_HEAD_DIM ** -0.5)
        v = jnp.dot(memb, w_mv_ref[...], preferred_element_type=F32)
        kshape = (D_MEM, N_MEM_HEADS * MEM_LEN)
        k_row_head = lax.broadcasted_iota(jnp.int32, kshape, 0) // MEM_HEAD_DIM
        k_col_head = lax.broadcasted_iota(jnp.int32, kshape, 1) // MEM_LEN
        kt4 = jnp.concatenate([kt] * N_MEM_HEADS, axis=1)
        kbd[...] = jnp.where(k_row_head == k_col_head, kt4, 0.0).astype(BF16)
        vshape = (N_MEM_HEADS * MEM_LEN, D_MEM)
        v_row_head = lax.broadcasted_iota(jnp.int32, vshape, 0) // MEM_LEN
        v_col_head = lax.broadcasted_iota(jnp.int32, vshape, 1) // MEM_HEAD_DIM
        v4 = jnp.concatenate([v] * N_MEM_HEADS, axis=0)
        same_head = v_row_head == v_col_head
        vbd[:, 0:D_MEM] = jnp.where(same_head, v4, 0.0).astype(BF16)
        vbd[:, D_MEM:2 * D_MEM] = jnp.where(same_head, 1.0, 0.0).astype(BF16)

    st = [dict() for _ in range(TILES_PER_STEP)]

    def rows_of(t):
        return slice(t * tq, (t + 1) * tq)

    def front_x(t):
        x = x_ref[0, rows_of(t), :]
        if first_layer:
            x = _to_segment_major(_layer_norm(x, ln0_g_ref[...], ln0_b_ref[...]), tq)
        st[t].update(x=x, xb=x.astype(BF16))

    def front_a(t):
        h = jnp.dot(st[t]["xb"], w_in_ref[:, 0:2 * D_CONV], preferred_element_type=F32)
        a = h[:, 0:D_CONV] * jax.nn.sigmoid(h[:, D_CONV:2 * D_CONV])
        a_prev = atail[...] if t == 0 else st[t - 1]["a_tail"]
        a_tail = a[tq - conv_halo:tq, :]
        ebuf[t, 0:conv_halo, :] = _halo(a_tail, a_prev, amix.at[t])
        ebuf[t, conv_halo:conv_halo + tq, :] = a
        st[t].update(a_tail=a_tail)

    def front_p(t):
        p_in = jnp.dot(st[t]["xb"], w_in_ref[:, 2 * D_CONV:2 * D_CONV + D_POOL], preferred_element_type=F32)
        p_prev = ptail[...] if t == 0 else st[t - 1]["p_tail"]
        p_tail = p_in[tq - pool_halo:tq, :]
        pbuf[t, 0:pool_halo, :] = _halo(p_tail, p_prev, pmix.at[t])
        pbuf[t, pool_halo:pool_halo + tq, :] = p_in
        st[t].update(p_tail=p_tail)

    def front_q(t):
        q = jnp.dot(st[t]["xb"], w_in_ref[:, 2 * D_CONV + D_POOL:D_IN], preferred_element_type=F32)
        qbuf[t] = q.astype(BF16)

    def conv(t, r0):
        accs = [jnp.zeros((SUBLANES, D_CONV), F32)] * (ROW_CHUNK // SUBLANES)
        for j in range(CONV_K):
            wj = conv_w_ref[SUBLANES * j:SUBLANES * (j + 1), :]
            for k in range(len(accs)):
                lo = r0 + SUBLANES * (j + k)
                accs[k] = accs[k] + wj * ebuf[t, lo:lo + SUBLANES, :]
        acc = jnp.concatenate(accs, axis=0) + conv_b_ref[...]
        y = _layer_norm(acc, cln_g_ref[...], cln_b_ref[...])
        y = y * jax.nn.sigmoid(y)
        cat[t, r0:r0 + ROW_CHUNK, 0:D_CONV] = y.astype(BF16)

    def pool(t):
        ext = pbuf[t]
        p_in = ext[pool_halo:, :]
        wins = []
        for lane0 in range(0, D_POOL, LANES):
            lo_w = POOL_WINDOWS[lane0 // POOL_GROUP]
            hi_w = POOL_WINDOWS[(lane0 + LANES - 1) // POOL_GROUP]
            split = (lane0 // POOL_GROUP + 1) * POOL_GROUP - lane0
            s = ext[:, lane0:lane0 + LANES]
            sums = {}
            span = 1
            while span < hi_w:
                shift = SUBLANES * span
                s = s[shift:, :] + s[:-shift, :]
                span *= 2
                if span in (lo_w, hi_w):
                    sums[span] = s[s.shape[0] - tq:, :]
            lane = lax.broadcasted_iota(jnp.int32, (tq, LANES), 1)
            wins.append(jnp.where(lane < split, sums[lo_w], sums[hi_w]))
        win = jnp.concatenate(wins, axis=1)
        count = count_first_ref[0] if t == 0 else count_later_ref[0]
        d = (win / count - p_in).astype(BF16)
        pooled = jnp.dot(d, w_pool_ref[...], preferred_element_type=F32) * pool_scale_ref[...]
        cat[t, :, D_CONV:D_CONV + D_POOL] = pooled.astype(BF16)

    def attn(t):
        s = jnp.dot(qbuf[t], kbd[...], preferred_element_type=F32)
        es = []
        for hd in range(N_MEM_HEADS):
            sh = s[:, hd * MEM_LEN:(hd + 1) * MEM_LEN]
            es.append(jnp.exp(sh - jnp.max(sh, axis=-1, keepdims=True)))
        e = jnp.concatenate(es, axis=1).astype(BF16)
        ov = jnp.dot(e, vbd[...], preferred_element_type=F32)
        cat[t, :, D_CONV + D_POOL:] = (ov[:, 0:D_MEM] / ov[:, D_MEM:2 * D_MEM]).astype(BF16)

    def out(t):
        y = jnp.dot(cat[t], w_out_ref[...], preferred_element_type=F32)
        o_ref[0, rows_of(t), :] = _layer_norm(DEEPNORM_ALPHA * st[t]["x"] + y, ln_g_ref[...], ln_b_ref[...])

    chunks = list(range(0, tq, ROW_CHUNK))
    n = len(chunks)

    def convs(t, lo, hi):
        for r0 in chunks[n * lo // 8:n * hi // 8]:
            conv(t, r0)

    front_x(0); front_a(0); front_p(0); front_q(0)
    front_x(1)
    convs(0, 0, 1); front_a(1)
    convs(0, 1, 3); front_p(1)
    convs(0, 3, 4); front_q(1)
    convs(0, 4, 5); pool(0)
    convs(0, 5, 6); attn(0)
    convs(0, 6, 8)
    convs(1, 0, 2); out(0)
    convs(1, 2, 8)
    pool(1); attn(1); out(1)

    atail[...] = st[TILES_PER_STEP - 1]["a_tail"]
    ptail[...] = st[TILES_PER_STEP - 1]["p_tail"]


def _ff_chunks():
    chunks = []
    c0 = 0
    while c0 < D_FF:
        cn = min(FF_CHUNK, D_FF - c0)
        chunks.append((c0, cn))
        c0 += cn
    return chunks


def _ffn_kernel(x_ref, w_up_ref, conv_w_ref, conv_b_ref, w_down_ref, ln_g_ref, ln_b_ref,
                o_ref, utail, umix, act, *, tq, last_layer):
    i = pl.program_id(1)
    halo = SUBLANES * FFN_BACK

    @pl.when(i == 0)
    def _start_of_sequence():
        utail[...] = jnp.zeros(utail.shape, F32)

    u_prev = {}
    for t in range(TILES_PER_STEP):
        rows = slice(t * tq, (t + 1) * tq)
        x = x_ref[0, rows, :]
        xb = x.astype(BF16)

        def conv_part(col, cn):
            cols = slice(col, col + cn)
            u = jnp.dot(xb, w_up_ref[:, cols], preferred_element_type=F32)
            u_tail = u[tq - halo:tq, :]
            prev = utail[:, cols] if t == 0 else u_prev[col]
            ext = jnp.concatenate([_halo(u_tail, prev, umix.at[t, :, cols]), u], axis=0)
            u_prev[col] = u_tail
            y = conv_b_ref[:, cols] + conv_w_ref[FFN_BACK:FFN_K, cols] * u
            for k in range(FFN_BACK):
                y = y + conv_w_ref[k:k + 1, cols] * ext[SUBLANES * k:SUBLANES * k + tq, :]
            return y

        for c0, cn in _ff_chunks():
            g = conv_part(c0, cn)
            v = conv_part(D_FF + c0, cn)
            act[t, :, c0:c0 + cn] = (g * jax.nn.sigmoid(g) * v).astype(BF16)

        y = jnp.dot(act[t], w_down_ref[...], preferred_element_type=F32)
        out = _layer_norm(DEEPNORM_ALPHA * x + y, ln_g_ref[...], ln_b_ref[...])
        o_ref[0, rows, :] = _to_frame_order(out, tq) if last_layer else out

    for col, u_tail in u_prev.items():
        utail[:, col:col + u_tail.shape[1]] = u_tail


def _whole(shape):
    return pl.BlockSpec(shape, lambda b, i: (0,) * len(shape), pipeline_mode=pl.Buffered(1))


def _row(v):
    return v.reshape(1, -1)


def _params():
    return pltpu.CompilerParams(dimension_semantics=("arbitrary", "arbitrary"),
                                vmem_limit_bytes=VMEM_LIMIT_BYTES)


def _pool_counts(tq):
    rows = jnp.arange(tq)
    frame = (rows % SUBLANES) * (tq // SUBLANES) + rows // SUBLANES
    window = jnp.repeat(jnp.asarray(POOL_WINDOWS, jnp.int32), POOL_GROUP)
    first = jnp.minimum(frame[:, None] + 1, window[None, :])
    later = jnp.broadcast_to(window[None, :], (tq, D_POOL))
    return jnp.stack([first, later]).astype(F32)


def _mixer_call(x, mem, ln0_g, ln0_b, w_in, conv_w, conv_b, cln_g, cln_b, w_pool_bd, pool_scale, w_mkt, w_mv, w_out,
                ln_g, ln_b, tq, first_layer):
    bsz, seq, d = x.shape
    step_rows = TILES_PER_STEP * tq
    tile = pl.BlockSpec((1, step_rows, d), lambda b_, i: (b_, i, 0))
    mem_spec = pl.BlockSpec((1, MEM_LEN, d), lambda b_, i: (b_, 0, 0))
    count_first = pl.BlockSpec((1, tq, D_POOL), lambda b_, i: (jnp.minimum(i, 1), 0, 0))
    count_later = pl.BlockSpec((1, tq, D_POOL), lambda b_, i: (1, 0, 0), pipeline_mode=pl.Buffered(1))
    counts = _pool_counts(tq)
    conv_w_rows = jnp.repeat(conv_w, SUBLANES, axis=0)
    args = [x, mem, _row(ln0_g), _row(ln0_b), w_in, conv_w_rows, _row(conv_b), _row(cln_g), _row(cln_b), w_pool_bd,
            _row(pool_scale), counts, counts, w_mkt, w_mv, w_out, _row(ln_g), _row(ln_b)]
    in_specs = [tile, mem_spec] + [_whole(a.shape) for a in args[2:]]
    in_specs[11], in_specs[12] = count_first, count_later
    return pl.pallas_call(
        functools.partial(_mixer_kernel, tq=tq, first_layer=first_layer),
        grid=(bsz, seq // step_rows),
        in_specs=in_specs,
        out_specs=tile,
        out_shape=jax.ShapeDtypeStruct(x.shape, x.dtype),
        scratch_shapes=[
            pltpu.VMEM((TILES_PER_STEP, SUBLANES * CONV_BACK + tq, D_CONV), F32),
            pltpu.VMEM((TILES_PER_STEP, SUBLANES * POOL_BACK + tq, D_POOL), F32),
            pltpu.VMEM((TILES_PER_STEP, tq, D_MEM), BF16),
            pltpu.VMEM((SUBLANES * CONV_BACK, D_CONV), F32),
            pltpu.VMEM((SUBLANES * POOL_BACK, D_POOL), F32),
            pltpu.VMEM((TILES_PER_STEP, SUBLANES * (CONV_BACK + 1), D_CONV), F32),
            pltpu.VMEM((TILES_PER_STEP, SUBLANES * (POOL_BACK + 1), D_POOL), F32),
            pltpu.VMEM((D_MEM, N_MEM_HEADS * MEM_LEN), BF16),
            pltpu.VMEM((N_MEM_HEADS * MEM_LEN, 2 * D_MEM), BF16),
            pltpu.VMEM((TILES_PER_STEP, tq, D_MODEL), BF16),
        ],
        compiler_params=_params(),
        name="mixer",
    )(*args)


def _ffn_call(x, w_up, conv_w, conv_b, w_down, ln_g, ln_b, tq, last_layer):
    bsz, seq, d = x.shape
    step_rows = TILES_PER_STEP * tq
    tile = pl.BlockSpec((1, step_rows, d), lambda b_, i: (b_, i, 0))
    args = (x, w_up, conv_w, _row(conv_b), w_down, _row(ln_g), _row(ln_b))
    in_specs = [tile] + [_whole(a.shape) for a in args[1:]]
    return pl.pallas_call(
        functools.partial(_ffn_kernel, tq=tq, last_layer=last_layer),
        grid=(bsz, seq // step_rows),
        in_specs=in_specs,
        out_specs=tile,
        out_shape=jax.ShapeDtypeStruct(x.shape, x.dtype),
        scratch_shapes=[
            pltpu.VMEM((SUBLANES * FFN_BACK, 2 * D_FF), F32),
            pltpu.VMEM((TILES_PER_STEP, SUBLANES * (FFN_BACK + 1), 2 * D_FF), F32),
            pltpu.VMEM((TILES_PER_STEP, tq, D_FF), BF16),
        ],
        compiler_params=_params(),
        name="ffn",
    )(*args)


def _block_diag(w):
    g, n, _ = w.shape
    eye = jnp.eye(g, dtype=w.dtype)
    return jnp.einsum("gh,gcd->gchd", eye, w).reshape(g * n, g * n)


def kernel(x, mem, ln0_g, ln0_b, w_in, conv_w, conv_b, conv_ln_g, conv_ln_b, pool_w, pool_scale, w_mk, w_mv, w_out, ln1_g, ln1_b, w_up, ffn_conv_w, ffn_conv_b, w_down, ln2_g, ln2_b):
    bsz, seq, d = x.shape
    assert d == D_MODEL and mem.shape == (bsz, MEM_LEN, D_MODEL)
    tq = ROW_TILE
    assert TILES_PER_STEP == 2, "the mixer's program order is written out for two tiles per step"
    assert seq % (TILES_PER_STEP * tq) == 0 and tq % ROW_CHUNK == 0 and tq // SUBLANES > CONV_BACK
    for l in range(DEPTH):
        x = _mixer_call(
            x, mem, ln0_g, ln0_b, w_in[l].astype(BF16), conv_w[l], conv_b[l], conv_ln_g[l], conv_ln_b[l],
            _block_diag(pool_w[l]).astype(BF16), pool_scale[l], w_mk[l].T.astype(BF16), w_mv[l].astype(BF16),
            w_out[l].astype(BF16), ln1_g[l], ln1_b[l], tq, first_layer=(l == 0))
        x = _ffn_call(x, w_up[l].astype(BF16), ffn_conv_w[l], ffn_conv_b[l], w_down[l].astype(BF16),
                      ln2_g[l], ln2_b[l], tq, last_layer=(l == DEPTH - 1))
    return x
```

```python
import functools

import jax
import jax.numpy as jnp
from jax import lax
from jax.experimental import pallas as pl
from jax.experimental.pallas import tpu as pltpu

D_MODEL = 1024
DEPTH = 4
MEM_LEN = 256
D_CONV = 384
CONV_K = 31
D_POOL = 384
POOL_WINDOWS = (2, 4, 8, 16)
POOL_GROUP = D_POOL // len(POOL_WINDOWS)
D_MEM = 256
N_MEM_HEADS = 4
MEM_HEAD_DIM = D_MEM // N_MEM_HEADS
D_IN = 2 * D_CONV + D_POOL + D_MEM
D_FF = 2816
FFN_K = 3
LN_EPS = 1e-5
DEEPNORM_ALPHA = (2.0 * DEPTH) ** 0.25

SUBLANES = 8
LANES = 128

ROW_TILE = 256
TILES_PER_STEP = 2
CONV_BACK = CONV_K - 1
POOL_BACK = max(POOL_WINDOWS) - 1
FFN_BACK = FFN_K - 1
ROW_CHUNK = 32
FF_CHUNK = 512
VMEM_LIMIT_BYTES = 56 * 1024 * 1024

F32 = jnp.float32
BF16 = jnp.bfloat16


def _layer_norm(z, g, b):
    mu = jnp.mean(z, axis=-1, keepdims=True)
    zc = z - mu
    var = jnp.mean(zc * zc, axis=-1, keepdims=True)
    return zc * lax.rsqrt(var + LN_EPS) * g + b


def _to_segment_major(x, tq):
    return jnp.swapaxes(x.reshape(SUBLANES, tq // SUBLANES, x.shape[-1]), 0, 1).reshape(tq, x.shape[-1])


def _to_frame_order(x, tq):
    return jnp.swapaxes(x.reshape(tq // SUBLANES, SUBLANES, x.shape[-1]), 0, 1).reshape(tq, x.shape[-1])


def _halo(cur_tail, prev_tail, mbuf):
    rows, cols = cur_tail.shape
    zeros = jnp.zeros((SUBLANES, cols), F32)
    sub = lax.broadcasted_iota(jnp.int32, (rows + SUBLANES, cols), 0) % SUBLANES
    mbuf[...] = jnp.where(sub == SUBLANES - 1,
                          jnp.concatenate([prev_tail, zeros], axis=0),
                          jnp.concatenate([zeros, cur_tail], axis=0))
    return mbuf[SUBLANES - 1:SUBLANES - 1 + rows, :]


def _mixer_kernel(x_ref, mem_ref, ln0_g_ref, ln0_b_ref, w_in_ref, conv_w_ref, conv_b_ref, cln_g_ref, cln_b_ref,
                  w_pool_ref, pool_scale_ref, count_first_ref, count_later_ref, w_mkt_ref, w_mv_ref, w_out_ref,
                  ln_g_ref, ln_b_ref, o_ref, cbuf, ebuf, pbuf, qbuf, atail, ptail, amix, pmix, kbd, vbd, cat,
                  *, tq, first_layer):
    i = pl.program_id(1)
    conv_halo = SUBLANES * CONV_BACK
    pool_halo = SUBLANES * POOL_BACK

    @pl.when(i == 0)
    def _start_of_sequence():
        atail[...] = jnp.zeros(atail.shape, F32)
        ptail[...] = jnp.zeros(ptail.shape, F32)
        memb = mem_ref[0].astype(BF16)
        kt = lax.dot_general(w_mkt_ref[...], memb, (((1,), (1,)), ((), ())),
                             preferred_element_type=F32) * (MEM_HEAD_DIM ** -0.5)
        v = jnp.dot(memb, w_mv_ref[...], preferred_element_type=F32)
        kshape = (D_MEM, N_MEM_HEADS * MEM_LEN)
        k_row_head = lax.broadcasted_iota(jnp.int32, kshape, 0) // MEM_HEAD_DIM
        k_col_head = lax.broadcasted_iota(jnp.int32, kshape, 1) // MEM_LEN
        kt4 = jnp.concatenate([kt] * N_MEM_HEADS, axis=1)
        kbd[...] = jnp.where(k_row_head == k_col_head, kt4, 0.0).astype(BF16)
        vshape = (N_MEM_HEADS * MEM_LEN, D_MEM)
        v_row_head = lax.broadcasted_iota(jnp.int32, vshape, 0) // MEM_LEN
        v_col_head = lax.broadcasted_iota(jnp.int32, vshape, 1) // MEM_HEAD_DIM
        v4 = jnp.concatenate([v] * N_MEM_HEADS, axis=0)
        same_head = v_row_head == v_col_head
        vbd[:, 0:D_MEM] = jnp.where(same_head, v4, 0.0).astype(BF16)
        vbd[:, D_MEM:2 * D_MEM] = jnp.where(same_head, 1.0, 0.0).astype(BF16)

    st = [dict() for _ in range(TILES_PER_STEP)]

    def rows_of(t):
        return slice(t * tq, (t + 1) * tq)

    def front_x(t):
        x = x_ref[0, rows_of(t), :]
        if first_layer:
            x = _to_segment_major(_layer_norm(x, ln0_g_ref[...], ln0_b_ref[...]), tq)
        st[t].update(x=x, xb=x.astype(BF16))

    def front_a(t, l):
        lanes = slice(LANES * l, LANES * (l + 1))
        h = jnp.dot(st[t]["xb"], w_in_ref[:, 2 * LANES * l:2 * LANES * (l + 1)], preferred_element_type=F32)
        a = h[:, 0:LANES] * jax.nn.sigmoid(h[:, LANES:2 * LANES])
        a_prev = atail[:, lanes] if t == 0 else st[t - 1]["a_tail"][l]
        a_tail = a[tq - conv_halo:tq, :]
        ebuf[t, 0:conv_halo, lanes] = _halo(a_tail, a_prev, amix.at[t, l])
        ebuf[t, conv_halo:conv_halo + tq, lanes] = a
        st[t].setdefault("a_tail", {})[l] = a_tail

    def front_p(t):
        p_in = jnp.dot(st[t]["xb"], w_in_ref[:, 2 * D_CONV:2 * D_CONV + D_POOL], preferred_element_type=F32)
        p_prev = ptail[...] if t == 0 else st[t - 1]["p_tail"]
        p_tail = p_in[tq - pool_halo:tq, :]
        pbuf[t, 0:pool_halo, :] = _halo(p_tail, p_prev, pmix.at[t])
        pbuf[t, pool_halo:pool_halo + tq, :] = p_in
        st[t].update(p_tail=p_tail)

    def front_q(t):
        q = jnp.dot(st[t]["xb"], w_in_ref[:, 2 * D_CONV + D_POOL:D_IN], preferred_element_type=F32)
        qbuf[t] = q.astype(BF16)

    def conv(t, l, r0):
        lanes = slice(LANES * l, LANES * (l + 1))
        accs = [jnp.zeros((SUBLANES, LANES), F32)] * (ROW_CHUNK // SUBLANES)
        for j in range(CONV_K):
            wj = conv_w_ref[SUBLANES * j:SUBLANES * (j + 1), lanes]
            for k in range(len(accs)):
                lo = r0 + SUBLANES * (j + k)
                accs[k] = accs[k] + wj * ebuf[t, lo:lo + SUBLANES, lanes]
        cbuf[t, r0:r0 + ROW_CHUNK, lanes] = jnp.concatenate(accs, axis=0) + conv_b_ref[:, lanes]

    def conv_norm(t):
        y = _layer_norm(cbuf[t], cln_g_ref[...], cln_b_ref[...])
        y = y * jax.nn.sigmoid(y)
        cat[t, :, 0:D_CONV] = y.astype(BF16)

    def pool(t):
        ext = pbuf[t]
        p_in = ext[pool_halo:, :]
        wins = []
        for lane0 in range(0, D_POOL, LANES):
            lo_w = POOL_WINDOWS[lane0 // POOL_GROUP]
            hi_w = POOL_WINDOWS[(lane0 + LANES - 1) // POOL_GROUP]
            split = (lane0 // POOL_GROUP + 1) * POOL_GROUP - lane0
            s = ext[:, lane0:lane0 + LANES]
            sums = {}
            span = 1
            while span < hi_w:
                shift = SUBLANES * span
                s = s[shift:, :] + s[:-shift, :]
                span *= 2
                if span in (lo_w, hi_w):
                    sums[span] = s[s.shape[0] - tq:, :]
            lane = lax.broadcasted_iota(jnp.int32, (tq, LANES), 1)
            wins.append(jnp.where(lane < split, sums[lo_w], sums[hi_w]))
        win = jnp.concatenate(wins, axis=1)
        count = count_first_ref[0] if t == 0 else count_later_ref[0]
        d = (win / count - p_in).astype(BF16)
        pooled = jnp.dot(d, w_pool_ref[...], preferred_element_type=F32) * pool_scale_ref[...]
        cat[t, :, D_CONV:D_CONV + D_POOL] = pooled.astype(BF16)

    def attn(t):
        s = jnp.dot(qbuf[t], kbd[...], preferred_element_type=F32)
        es = []
        for hd in range(N_MEM_HEADS):
            sh = s[:, hd * MEM_LEN:(hd + 1) * MEM_LEN]
            es.append(jnp.exp(sh - jnp.max(sh, axis=-1, keepdims=True)))
        e = jnp.concatenate(es, axis=1).astype(BF16)
        ov = jnp.dot(e, vbd[...], preferred_element_type=F32)
        cat[t, :, D_CONV + D_POOL:] = (ov[:, 0:D_MEM] / ov[:, D_MEM:2 * D_MEM]).astype(BF16)

    def out(t):
        y = jnp.dot(cat[t], w_out_ref[...], preferred_element_type=F32)
        o_ref[0, rows_of(t), :] = _layer_norm(DEEPNORM_ALPHA * st[t]["x"] + y, ln_g_ref[...], ln_b_ref[...])

    n_lane = D_CONV // LANES

    def convs(t, l):
        for r0 in range(0, tq, ROW_CHUNK):
            conv(t, l, r0)

    front_x(0); front_a(0, 0); front_a(0, 1)
    convs(0, 0); front_a(0, 2)
    convs(0, 1); front_p(0); front_q(0); front_x(1)
    convs(0, 2); front_a(1, 0); front_a(1, 1)
    conv_norm(0); pool(0)
    convs(1, 0); attn(0); front_a(1, 2)
    convs(1, 1); out(0); front_p(1); front_q(1)
    convs(1, 2)
    conv_norm(1); pool(1); attn(1); out(1)

    last = st[TILES_PER_STEP - 1]
    for l in range(n_lane):
        atail[:, LANES * l:LANES * (l + 1)] = last["a_tail"][l]
    ptail[...] = last["p_tail"]


def _ff_chunks():
    chunks = []
    c0 = 0
    while c0 < D_FF:
        cn = min(FF_CHUNK, D_FF - c0)
        chunks.append((c0, cn))
        c0 += cn
    return chunks


def _ffn_kernel(x_ref, w_up_ref, conv_w_ref, conv_b_ref, w_down_ref, ln_g_ref, ln_b_ref,
                o_ref, utail, umix, act, *, tq, last_layer):
    i = pl.program_id(1)
    halo = SUBLANES * FFN_BACK

    @pl.when(i == 0)
    def _start_of_sequence():
        utail[...] = jnp.zeros(utail.shape, F32)

    u_prev = {}
    for t in range(TILES_PER_STEP):
        rows = slice(t * tq, (t + 1) * tq)
        x = x_ref[0, rows, :]
        xb = x.astype(BF16)

        def conv_part(col, cn):
            cols = slice(col, col + cn)
            u = jnp.dot(xb, w_up_ref[:, cols], preferred_element_type=F32)
            u_tail = u[tq - halo:tq, :]
            prev = utail[:, cols] if t == 0 else u_prev[col]
            ext = jnp.concatenate([_halo(u_tail, prev, umix.at[t, :, cols]), u], axis=0)
            u_prev[col] = u_tail
            y = conv_b_ref[:, cols] + conv_w_ref[FFN_BACK:FFN_K, cols] * u
            for k in range(FFN_BACK):
                y = y + conv_w_ref[k:k + 1, cols] * ext[SUBLANES * k:SUBLANES * k + tq, :]
            return y

        for c0, cn in _ff_chunks():
            g = conv_part(c0, cn)
            v = conv_part(D_FF + c0, cn)
            act[t, :, c0:c0 + cn] = (g * jax.nn.sigmoid(g) * v).astype(BF16)

        y = jnp.dot(act[t], w_down_ref[...], preferred_element_type=F32)
        out = _layer_norm(DEEPNORM_ALPHA * x + y, ln_g_ref[...], ln_b_ref[...])
        o_ref[0, rows, :] = _to_frame_order(out, tq) if last_layer else out

    for col, u_tail in u_prev.items():
        utail[:, col:col + u_tail.shape[1]] = u_tail


def _whole(shape):
    return pl.BlockSpec(shape, lambda b, i: (0,) * len(shape), pipeline_mode=pl.Buffered(1))


def _row(v):
    return v.reshape(1, -1)


def _params():
    return pltpu.CompilerParams(dimension_semantics=("arbitrary", "arbitrary"),
                                vmem_limit_bytes=VMEM_LIMIT_BYTES)


def _pool_counts(tq):
    rows = jnp.arange(tq)
    frame = (rows % SUBLANES) * (tq // SUBLANES) + rows // SUBLANES
    window = jnp.repeat(jnp.asarray(POOL_WINDOWS, jnp.int32), POOL_GROUP)
    first = jnp.minimum(frame[:, None] + 1, window[None, :])
    later = jnp.broadcast_to(window[None, :], (tq, D_POOL))
    return jnp.stack([first, later]).astype(F32)


def _mixer_call(x, mem, ln0_g, ln0_b, w_in, conv_w, conv_b, cln_g, cln_b, w_pool_bd, pool_scale, w_mkt, w_mv, w_out,
                ln_g, ln_b, tq, first_layer):
    bsz, seq, d = x.shape
    step_rows = TILES_PER_STEP * tq
    tile = pl.BlockSpec((1, step_rows, d), lambda b_, i: (b_, i, 0))
    mem_spec = pl.BlockSpec((1, MEM_LEN, d), lambda b_, i: (b_, 0, 0))
    count_first = pl.BlockSpec((1, tq, D_POOL), lambda b_, i: (jnp.minimum(i, 1), 0, 0))
    count_later = pl.BlockSpec((1, tq, D_POOL), lambda b_, i: (1, 0, 0), pipeline_mode=pl.Buffered(1))
    counts = _pool_counts(tq)
    conv_w_rows = jnp.repeat(conv_w, SUBLANES, axis=0)
    args = [x, mem, _row(ln0_g), _row(ln0_b), w_in, conv_w_rows, _row(conv_b), _row(cln_g), _row(cln_b), w_pool_bd,
            _row(pool_scale), counts, counts, w_mkt, w_mv, w_out, _row(ln_g), _row(ln_b)]
    in_specs = [tile, mem_spec] + [_whole(a.shape) for a in args[2:]]
    in_specs[11], in_specs[12] = count_first, count_later
    return pl.pallas_call(
        functools.partial(_mixer_kernel, tq=tq, first_layer=first_layer),
        grid=(bsz, seq // step_rows),
        in_specs=in_specs,
        out_specs=tile,
        out_shape=jax.ShapeDtypeStruct(x.shape, x.dtype),
        scratch_shapes=[
            pltpu.VMEM((TILES_PER_STEP, tq, D_CONV), F32),
            pltpu.VMEM((TILES_PER_STEP, SUBLANES * CONV_BACK + tq, D_CONV), F32),
            pltpu.VMEM((TILES_PER_STEP, SUBLANES * POOL_BACK + tq, D_POOL), F32),
            pltpu.VMEM((TILES_PER_STEP, tq, D_MEM), BF16),
            pltpu.VMEM((SUBLANES * CONV_BACK, D_CONV), F32),
            pltpu.VMEM((SUBLANES * POOL_BACK, D_POOL), F32),
            pltpu.VMEM((TILES_PER_STEP, D_CONV // LANES, SUBLANES * (CONV_BACK + 1), LANES), F32),
            pltpu.VMEM((TILES_PER_STEP, SUBLANES * (POOL_BACK + 1), D_POOL), F32),
            pltpu.VMEM((D_MEM, N_MEM_HEADS * MEM_LEN), BF16),
            pltpu.VMEM((N_MEM_HEADS * MEM_LEN, 2 * D_MEM), BF16),
            pltpu.VMEM((TILES_PER_STEP, tq, D_MODEL), BF16),
        ],
        compiler_params=_params(),
        name="mixer",
    )(*args)


def _ffn_call(x, w_up, conv_w, conv_b, w_down, ln_g, ln_b, tq, last_layer):
    bsz, seq, d = x.shape
    step_rows = TILES_PER_STEP * tq
    tile = pl.BlockSpec((1, step_rows, d), lambda b_, i: (b_, i, 0))
    args = (x, w_up, conv_w, _row(conv_b), w_down, _row(ln_g), _row(ln_b))
    in_specs = [tile] + [_whole(a.shape) for a in args[1:]]
    return pl.pallas_call(
        functools.partial(_ffn_kernel, tq=tq, last_layer=last_layer),
        grid=(bsz, seq // step_rows),
        in_specs=in_specs,
        out_specs=tile,
        out_shape=jax.ShapeDtypeStruct(x.shape, x.dtype),
        scratch_shapes=[
            pltpu.VMEM((SUBLANES * FFN_BACK, 2 * D_FF), F32),
            pltpu.VMEM((TILES_PER_STEP, SUBLANES * (FFN_BACK + 1), 2 * D_FF), F32),
            pltpu.VMEM((TILES_PER_STEP, tq, D_FF), BF16),
        ],
        compiler_params=_params(),
        name="ffn",
    )(*args)


def _pair_glu_columns(w_in):
    pieces = []
    for l in range(D_CONV // LANES):
        pieces += [w_in[:, LANES * l:LANES * (l + 1)], w_in[:, D_CONV + LANES * l:D_CONV + LANES * (l + 1)]]
    return jnp.concatenate(pieces + [w_in[:, 2 * D_CONV:]], axis=1)


def _block_diag(w):
    g, n, _ = w.shape
    eye = jnp.eye(g, dtype=w.dtype)
    return jnp.einsum("gh,gcd->gchd", eye, w).reshape(g * n, g * n)


def kernel(x, mem, ln0_g, ln0_b, w_in, conv_w, conv_b, conv_ln_g, conv_ln_b, pool_w, pool_scale, w_mk, w_mv, w_out, ln1_g, ln1_b, w_up, ffn_conv_w, ffn_conv_b, w_down, ln2_g, ln2_b):
    bsz, seq, d = x.shape
    assert d == D_MODEL and mem.shape == (bsz, MEM_LEN, D_MODEL)
    tq = ROW_TILE
    assert TILES_PER_STEP == 2, "the mixer's program order is written out for two tiles per step"
    assert seq % (TILES_PER_STEP * tq) == 0 and tq % ROW_CHUNK == 0 and tq // SUBLANES > CONV_BACK
    for l in range(DEPTH):
        x = _mixer_call(
            x, mem, ln0_g, ln0_b, _pair_glu_columns(w_in[l]).astype(BF16), conv_w[l], conv_b[l], conv_ln_g[l],
            conv_ln_b[l],
            _block_diag(pool_w[l]).astype(BF16), pool_scale[l], w_mk[l].T.astype(BF16), w_mv[l].astype(BF16),
            w_out[l].astype(BF16), ln1_g[l], ln1_b[l], tq, first_layer=(l == 0))
        x = _ffn_call(x, w_up[l].astype(BF16), ffn_conv_w[l], ffn_conv_b[l], w_down[l].astype(BF16),
                      ln2_g[l], ln2_b[l], tq, last_layer=(l == DEPTH - 1))
    return x
```

```python
import functools

import jax
import jax.numpy as jnp
from jax import lax
from jax.experimental import pallas as pl
from jax.experimental.pallas import tpu as pltpu

D_MODEL = 1024
DEPTH = 4
MEM_LEN = 256
D_CONV = 384
CONV_K = 31
D_POOL = 384
POOL_WINDOWS = (2, 4, 8, 16)
POOL_GROUP = D_POOL // len(POOL_WINDOWS)
D_MEM = 256
N_MEM_HEADS = 4
MEM_HEAD_DIM = D_MEM // N_MEM_HEADS
D_IN = 2 * D_CONV + D_POOL + D_MEM
D_FF = 2816
FFN_K = 3
LN_EPS = 1e-5
DEEPNORM_ALPHA = (2.0 * DEPTH) ** 0.25

SUBLANES = 8
LANES = 128

ROW_TILE = 256
TILES_PER_STEP = 2
CONV_BACK = CONV_K - 1
POOL_BACK = max(POOL_WINDOWS) - 1
FFN_BACK = FFN_K - 1
ROW_CHUNK = 32
FF_CHUNK = 256
VMEM_LIMIT_BYTES = 56 * 1024 * 1024

F32 = jnp.float32
BF16 = jnp.bfloat16


def _layer_norm(z, g, b):
    mu = jnp.mean(z, axis=-1, keepdims=True)
    zc = z - mu
    var = jnp.mean(zc * zc, axis=-1, keepdims=True)
    return zc * lax.rsqrt(var + LN_EPS) * g + b


def _to_segment_major(x, tq):
    return jnp.swapaxes(x.reshape(SUBLANES, tq // SUBLANES, x.shape[-1]), 0, 1).reshape(tq, x.shape[-1])


def _to_frame_order(x, tq):
    return jnp.swapaxes(x.reshape(tq // SUBLANES, SUBLANES, x.shape[-1]), 0, 1).reshape(tq, x.shape[-1])


def _halo(cur_tail, prev_tail, mbuf):
    rows, cols = cur_tail.shape
    zeros = jnp.zeros((SUBLANES, cols), F32)
    sub = lax.broadcasted_iota(jnp.int32, (rows + SUBLANES, cols), 0) % SUBLANES
    mbuf[...] = jnp.where(sub == SUBLANES - 1,
                          jnp.concatenate([prev_tail, zeros], axis=0),
                          jnp.concatenate([zeros, cur_tail], axis=0))
    return mbuf[SUBLANES - 1:SUBLANES - 1 + rows, :]


def _mixer_kernel(x_ref, mem_ref, ln0_g_ref, ln0_b_ref, w_in_ref, conv_w_ref, conv_b_ref, cln_g_ref, cln_b_ref,
                  w_pool_ref, pool_scale_ref, count_first_ref, count_later_ref, w_mkt_ref, w_mv_ref, w_out_ref,
                  ln_g_ref, ln_b_ref, o_ref, cbuf, ebuf, pbuf, qbuf, atail, ptail, amix, pmix, kbd, vbd, cat,
                  *, tq, first_layer):
    i = pl.program_id(1)
    conv_halo = SUBLANES * CONV_BACK
    pool_halo = SUBLANES * POOL_BACK

    @pl.when(i == 0)
    def _start_of_sequence():
        atail[...] = jnp.zeros(atail.shape, F32)
        ptail[...] = jnp.zeros(ptail.shape, F32)
        memb = mem_ref[0].astype(BF16)
        kt = lax.dot_general(w_mkt_ref[...], memb, (((1,), (1,)), ((), ())),
                             preferred_element_type=F32) * (MEM_HEAD_DIM ** -0.5)
        v = jnp.dot(memb, w_mv_ref[...], preferred_element_type=F32)
        kshape = (D_MEM, N_MEM_HEADS * MEM_LEN)
        k_row_head = lax.broadcasted_iota(jnp.int32, kshape, 0) // MEM_HEAD_DIM
        k_col_head = lax.broadcasted_iota(jnp.int32, kshape, 1) // MEM_LEN
        kt4 = jnp.concatenate([kt] * N_MEM_HEADS, axis=1)
        kbd[...] = jnp.where(k_row_head == k_col_head, kt4, 0.0).astype(BF16)
        vshape = (N_MEM_HEADS * MEM_LEN, D_MEM)
        v_row_head = lax.broadcasted_iota(jnp.int32, vshape, 0) // MEM_LEN
        v_col_head = lax.broadcasted_iota(jnp.int32, vshape, 1) // MEM_HEAD_DIM
        v4 = jnp.concatenate([v] * N_MEM_HEADS, axis=0)
        same_head = v_row_head == v_col_head
        vbd[:, 0:D_MEM] = jnp.where(same_head, v4, 0.0).astype(BF16)
        vbd[:, D_MEM:2 * D_MEM] = jnp.where(same_head, 1.0, 0.0).astype(BF16)

    st = [dict() for _ in range(TILES_PER_STEP)]

    def rows_of(t):
        return slice(t * tq, (t + 1) * tq)

    def front_x(t):
        x = x_ref[0, rows_of(t), :]
        if first_layer:
            x = _to_segment_major(_layer_norm(x, ln0_g_ref[...], ln0_b_ref[...]), tq)
        st[t].update(x=x, xb=x.astype(BF16))

    def front_a(t, l):
        lanes = slice(LANES * l, LANES * (l + 1))
        h = jnp.dot(st[t]["xb"], w_in_ref[:, 2 * LANES * l:2 * LANES * (l + 1)], preferred_element_type=F32)
        a = h[:, 0:LANES] * jax.nn.sigmoid(h[:, LANES:2 * LANES])
        a_prev = atail[:, lanes] if t == 0 else st[t - 1]["a_tail"][l]
        a_tail = a[tq - conv_halo:tq, :]
        ebuf[t, 0:conv_halo, lanes] = _halo(a_tail, a_prev, amix.at[t, l])
        ebuf[t, conv_halo:conv_halo + tq, lanes] = a
        st[t].setdefault("a_tail", {})[l] = a_tail

    def front_p(t):
        p_in = jnp.dot(st[t]["xb"], w_in_ref[:, 2 * D_CONV:2 * D_CONV + D_POOL], preferred_element_type=F32)
        p_prev = ptail[...] if t == 0 else st[t - 1]["p_tail"]
        p_tail = p_in[tq - pool_halo:tq, :]
        pbuf[t, 0:pool_halo, :] = _halo(p_tail, p_prev, pmix.at[t])
        pbuf[t, pool_halo:pool_halo + tq, :] = p_in
        st[t].update(p_tail=p_tail)

    def front_q(t):
        q = jnp.dot(st[t]["xb"], w_in_ref[:, 2 * D_CONV + D_POOL:D_IN], preferred_element_type=F32)
        qbuf[t] = q.astype(BF16)

    def conv(t, l, r0):
        lanes = slice(LANES * l, LANES * (l + 1))
        accs = [jnp.zeros((SUBLANES, LANES), F32)] * (ROW_CHUNK // SUBLANES)
        for j in range(CONV_K):
            wj = conv_w_ref[SUBLANES * j:SUBLANES * (j + 1), lanes]
            for k in range(len(accs)):
                lo = r0 + SUBLANES * (j + k)
                accs[k] = accs[k] + wj * ebuf[t, lo:lo + SUBLANES, lanes]
        cbuf[t, r0:r0 + ROW_CHUNK, lanes] = jnp.concatenate(accs, axis=0) + conv_b_ref[:, lanes]

    def conv_norm(t):
        y = _layer_norm(cbuf[t], cln_g_ref[...], cln_b_ref[...])
        y = y * jax.nn.sigmoid(y)
        cat[t, :, 0:D_CONV] = y.astype(BF16)

    def pool(t):
        ext = pbuf[t]
        p_in = ext[pool_halo:, :]
        wins = []
        for lane0 in range(0, D_POOL, LANES):
            lo_w = POOL_WINDOWS[lane0 // POOL_GROUP]
            hi_w = POOL_WINDOWS[(lane0 + LANES - 1) // POOL_GROUP]
            split = (lane0 // POOL_GROUP + 1) * POOL_GROUP - lane0
            s = ext[:, lane0:lane0 + LANES]
            sums = {}
            span = 1
            while span < hi_w:
                shift = SUBLANES * span
                s = s[shift:, :] + s[:-shift, :]
                span *= 2
                if span in (lo_w, hi_w):
                    sums[span] = s[s.shape[0] - tq:, :]
            lane = lax.broadcasted_iota(jnp.int32, (tq, LANES), 1)
            wins.append(jnp.where(lane < split, sums[lo_w], sums[hi_w]))
        win = jnp.concatenate(wins, axis=1)
        count = count_first_ref[0] if t == 0 else count_later_ref[0]
        d = (win / count - p_in).astype(BF16)
        pooled = jnp.dot(d, w_pool_ref[...], preferred_element_type=F32) * pool_scale_ref[...]
        cat[t, :, D_CONV:D_CONV + D_POOL] = pooled.astype(BF16)

    def attn(t):
        s = jnp.dot(qbuf[t], kbd[...], preferred_element_type=F32)
        es = []
        for hd in range(N_MEM_HEADS):
            sh = s[:, hd * MEM_LEN:(hd + 1) * MEM_LEN]
            es.append(jnp.exp(sh - jnp.max(sh, axis=-1, keepdims=True)))
        e = jnp.concatenate(es, axis=1).astype(BF16)
        ov = jnp.dot(e, vbd[...], preferred_element_type=F32)
        cat[t, :, D_CONV + D_POOL:] = (ov[:, 0:D_MEM] / ov[:, D_MEM:2 * D_MEM]).astype(BF16)

    def out(t):
        y = jnp.dot(cat[t], w_out_ref[...], preferred_element_type=F32)
        o_ref[0, rows_of(t), :] = _layer_norm(DEEPNORM_ALPHA * st[t]["x"] + y, ln_g_ref[...], ln_b_ref[...])

    n_lane = D_CONV // LANES

    def convs(t, l):
        for r0 in range(0, tq, ROW_CHUNK):
            conv(t, l, r0)

    front_x(0); front_a(0, 0); front_a(0, 1)
    convs(0, 0); front_a(0, 2)
    convs(0, 1); front_p(0); front_q(0); front_x(1)
    convs(0, 2); front_a(1, 0); front_a(1, 1)
    conv_norm(0); pool(0)
    convs(1, 0); attn(0); front_a(1, 2)
    convs(1, 1); out(0); front_p(1); front_q(1)
    convs(1, 2)
    conv_norm(1); pool(1); attn(1); out(1)

    last = st[TILES_PER_STEP - 1]
    for l in range(n_lane):
        atail[:, LANES * l:LANES * (l + 1)] = last["a_tail"][l]
    ptail[...] = last["p_tail"]


def _ff_chunks():
    chunks = []
    c0 = 0
    while c0 < D_FF:
        cn = min(FF_CHUNK, D_FF - c0)
        chunks.append((c0, cn))
        c0 += cn
    return chunks


def _ffn_kernel(x_ref, w_up_ref, conv_w_ref, conv_b_ref, w_down_ref, ln_g_ref, ln_b_ref,
                o_ref, utail, umix, act, *, tq, last_layer):
    i = pl.program_id(1)
    halo = SUBLANES * FFN_BACK

    @pl.when(i == 0)
    def _start_of_sequence():
        utail[...] = jnp.zeros(utail.shape, F32)

    u_prev = {}
    for t in range(TILES_PER_STEP):
        rows = slice(t * tq, (t + 1) * tq)
        x = x_ref[0, rows, :]
        xb = x.astype(BF16)

        def conv_part(col, cn):
            cols = slice(col, col + cn)
            u = jnp.dot(xb, w_up_ref[:, cols], preferred_element_type=F32)
            u_tail = u[tq - halo:tq, :]
            prev = utail[:, cols] if t == 0 else u_prev[col]
            ext = jnp.concatenate([_halo(u_tail, prev, umix.at[t, col // FF_CHUNK]), u], axis=0)
            u_prev[col] = u_tail
            y = conv_b_ref[:, cols] + conv_w_ref[FFN_BACK:FFN_K, cols] * u
            for k in range(FFN_BACK):
                y = y + conv_w_ref[k:k + 1, cols] * ext[SUBLANES * k:SUBLANES * k + tq, :]
            return y

        for c0, cn in _ff_chunks():
            g = conv_part(c0, cn)
            v = conv_part(D_FF + c0, cn)
            act[t, :, c0:c0 + cn] = (g * jax.nn.sigmoid(g) * v).astype(BF16)

        y = jnp.dot(act[t], w_down_ref[...], preferred_element_type=F32)
        out = _layer_norm(DEEPNORM_ALPHA * x + y, ln_g_ref[...], ln_b_ref[...])
        o_ref[0, rows, :] = _to_frame_order(out, tq) if last_layer else out

    for col, u_tail in u_prev.items():
        utail[:, col:col + u_tail.shape[1]] = u_tail


def _whole(shape):
    return pl.BlockSpec(shape, lambda b, i: (0,) * len(shape), pipeline_mode=pl.Buffered(1))


def _row(v):
    return v.reshape(1, -1)


def _params():
    return pltpu.CompilerParams(dimension_semantics=("arbitrary", "arbitrary"),
                                vmem_limit_bytes=VMEM_LIMIT_BYTES)


def _pool_counts(tq):
    rows = jnp.arange(tq)
    frame = (rows % SUBLANES) * (tq // SUBLANES) + rows // SUBLANES
    window = jnp.repeat(jnp.asarray(POOL_WINDOWS, jnp.int32), POOL_GROUP)
    first = jnp.minimum(frame[:, None] + 1, window[None, :])
    later = jnp.broadcast_to(window[None, :], (tq, D_POOL))
    return jnp.stack([first, later]).astype(F32)


def _mixer_call(x, mem, ln0_g, ln0_b, w_in, conv_w, conv_b, cln_g, cln_b, w_pool_bd, pool_scale, w_mkt, w_mv, w_out,
                ln_g, ln_b, tq, first_layer):
    bsz, seq, d = x.shape
    step_rows = TILES_PER_STEP * tq
    tile = pl.BlockSpec((1, step_rows, d), lambda b_, i: (b_, i, 0))
    mem_spec = pl.BlockSpec((1, MEM_LEN, d), lambda b_, i: (b_, 0, 0))
    count_first = pl.BlockSpec((1, tq, D_POOL), lambda b_, i: (jnp.minimum(i, 1), 0, 0))
    count_later = pl.BlockSpec((1, tq, D_POOL), lambda b_, i: (1, 0, 0), pipeline_mode=pl.Buffered(1))
    counts = _pool_counts(tq)
    conv_w_rows = jnp.repeat(conv_w, SUBLANES, axis=0)
    args = [x, mem, _row(ln0_g), _row(ln0_b), w_in, conv_w_rows, _row(conv_b), _row(cln_g), _row(cln_b), w_pool_bd,
            _row(pool_scale), counts, counts, w_mkt, w_mv, w_out, _row(ln_g), _row(ln_b)]
    in_specs = [tile, mem_spec] + [_whole(a.shape) for a in args[2:]]
    in_specs[11], in_specs[12] = count_first, count_later
    return pl.pallas_call(
        functools.partial(_mixer_kernel, tq=tq, first_layer=first_layer),
        grid=(bsz, seq // step_rows),
        in_specs=in_specs,
        out_specs=tile,
        out_shape=jax.ShapeDtypeStruct(x.shape, x.dtype),
        scratch_shapes=[
            pltpu.VMEM((TILES_PER_STEP, tq, D_CONV), F32),
            pltpu.VMEM((TILES_PER_STEP, SUBLANES * CONV_BACK + tq, D_CONV), F32),
            pltpu.VMEM((TILES_PER_STEP, SUBLANES * POOL_BACK + tq, D_POOL), F32),
            pltpu.VMEM((TILES_PER_STEP, tq, D_MEM), BF16),
            pltpu.VMEM((SUBLANES * CONV_BACK, D_CONV), F32),
            pltpu.VMEM((SUBLANES * POOL_BACK, D_POOL), F32),
            pltpu.VMEM((TILES_PER_STEP, D_CONV // LANES, SUBLANES * (CONV_BACK + 1), LANES), F32),
            pltpu.VMEM((TILES_PER_STEP, SUBLANES * (POOL_BACK + 1), D_POOL), F32),
            pltpu.VMEM((D_MEM, N_MEM_HEADS * MEM_LEN), BF16),
            pltpu.VMEM((N_MEM_HEADS * MEM_LEN, 2 * D_MEM), BF16),
            pltpu.VMEM((TILES_PER_STEP, tq, D_MODEL), BF16),
        ],
        compiler_params=_params(),
        name="mixer",
    )(*args)


def _ffn_call(x, w_up, conv_w, conv_b, w_down, ln_g, ln_b, tq, last_layer):
    bsz, seq, d = x.shape
    step_rows = TILES_PER_STEP * tq
    tile = pl.BlockSpec((1, step_rows, d), lambda b_, i: (b_, i, 0))
    args = (x, w_up, conv_w, _row(conv_b), w_down, _row(ln_g), _row(ln_b))
    in_specs = [tile] + [_whole(a.shape) for a in args[1:]]
    return pl.pallas_call(
        functools.partial(_ffn_kernel, tq=tq, last_layer=last_layer),
        grid=(bsz, seq // step_rows),
        in_specs=in_specs,
        out_specs=tile,
        out_shape=jax.ShapeDtypeStruct(x.shape, x.dtype),
        scratch_shapes=[
            pltpu.VMEM((SUBLANES * FFN_BACK, 2 * D_FF), F32),
            pltpu.VMEM((TILES_PER_STEP, 2 * D_FF // FF_CHUNK, SUBLANES * (FFN_BACK + 1), FF_CHUNK), F32),
            pltpu.VMEM((TILES_PER_STEP, tq, D_FF), BF16),
        ],
        compiler_params=_params(),
        name="ffn",
    )(*args)


def _pair_glu_columns(w_in):
    pieces = []
    for l in range(D_CONV // LANES):
        pieces += [w_in[:, LANES * l:LANES * (l + 1)], w_in[:, D_CONV + LANES * l:D_CONV + LANES * (l + 1)]]
    return jnp.concatenate(pieces + [w_in[:, 2 * D_CONV:]], axis=1)


def _block_diag(w):
    g, n, _ = w.shape
    eye = jnp.eye(g, dtype=w.dtype)
    return jnp.einsum("gh,gcd->gchd", eye, w).reshape(g * n, g * n)


def kernel(x, mem, ln0_g, ln0_b, w_in, conv_w, conv_b, conv_ln_g, conv_ln_b, pool_w, pool_scale, w_mk, w_mv, w_out, ln1_g, ln1_b, w_up, ffn_conv_w, ffn_conv_b, w_down, ln2_g, ln2_b):
    bsz, seq, d = x.shape
    assert d == D_MODEL and mem.shape == (bsz, MEM_LEN, D_MODEL)
    tq = ROW_TILE
    assert TILES_PER_STEP == 2, "the mixer's program order is written out for two tiles per step"
    assert D_FF % FF_CHUNK == 0, "halo staging is indexed by whole chunks"
    assert seq % (TILES_PER_STEP * tq) == 0 and tq % ROW_CHUNK == 0 and tq // SUBLANES > CONV_BACK
    for l in range(DEPTH):
        x = _mixer_call(
            x, mem, ln0_g, ln0_b, _pair_glu_columns(w_in[l]).astype(BF16), conv_w[l], conv_b[l], conv_ln_g[l],
            conv_ln_b[l],
            _block_diag(pool_w[l]).astype(BF16), pool_scale[l], w_mk[l].T.astype(BF16), w_mv[l].astype(BF16),
            w_out[l].astype(BF16), ln1_g[l], ln1_b[l], tq, first_layer=(l == 0))
        x = _ffn_call(x, w_up[l].astype(BF16), ffn_conv_w[l], ffn_conv_b[l], w_down[l].astype(BF16),
                      ln2_g[l], ln2_b[l], tq, last_layer=(l == DEPTH - 1))
    return x
```

```python
import functools

import jax
import jax.numpy as jnp
from jax import lax
from jax.experimental import pallas as pl
from jax.experimental.pallas import tpu as pltpu

D_MODEL = 1024
DEPTH = 4
MEM_LEN = 256
D_CONV = 384
CONV_K = 31
D_POOL = 384
POOL_WINDOWS = (2, 4, 8, 16)
POOL_GROUP = D_POOL // len(POOL_WINDOWS)
D_MEM = 256
N_MEM_HEADS = 4
MEM_HEAD_DIM = D_MEM // N_MEM_HEADS
D_IN = 2 * D_CONV + D_POOL + D_MEM
D_FF = 2816
FFN_K = 3
LN_EPS = 1e-5
DEEPNORM_ALPHA = (2.0 * DEPTH) ** 0.25

SUBLANES = 8
LANES = 128

ROW_TILE = 256
TILES_PER_STEP = 2
CONV_BACK = CONV_K - 1
POOL_BACK = max(POOL_WINDOWS) - 1
FFN_BACK = FFN_K - 1
ROW_CHUNK = 32
FF_CHUNK = 256
VMEM_LIMIT_BYTES = 56 * 1024 * 1024

F32 = jnp.float32
BF16 = jnp.bfloat16


def _layer_norm(z, g, b):
    mu = jnp.mean(z, axis=-1, keepdims=True)
    zc = z - mu
    var = jnp.mean(zc * zc, axis=-1, keepdims=True)
    return zc * lax.rsqrt(var + LN_EPS) * g + b


def _to_segment_major(x, tq):
    return jnp.swapaxes(x.reshape(SUBLANES, tq // SUBLANES, x.shape[-1]), 0, 1).reshape(tq, x.shape[-1])


def _to_frame_order(x, tq):
    return jnp.swapaxes(x.reshape(tq // SUBLANES, SUBLANES, x.shape[-1]), 0, 1).reshape(tq, x.shape[-1])


def _halo(cur_tail, prev_tail, mbuf):
    rows, cols = cur_tail.shape
    zeros = jnp.zeros((SUBLANES, cols), F32)
    sub = lax.broadcasted_iota(jnp.int32, (rows + SUBLANES, cols), 0) % SUBLANES
    mbuf[...] = jnp.where(sub == SUBLANES - 1,
                          jnp.concatenate([prev_tail, zeros], axis=0),
                          jnp.concatenate([zeros, cur_tail], axis=0))
    return mbuf[SUBLANES - 1:SUBLANES - 1 + rows, :]


def _mixer_kernel(x_ref, mem_ref, ln0_g_ref, ln0_b_ref, w_in_ref, conv_w_ref, conv_b_ref, cln_g_ref, cln_b_ref,
                  w_pool_ref, pool_scale_ref, count_first_ref, count_later_ref, w_mkt_ref, w_mv_ref, w_out_ref,
                  ln_g_ref, ln_b_ref, o_ref, cbuf, ebuf, pbuf, qbuf, atail, ptail, amix, pmix, kbd, vbd, cat,
                  *, tq, first_layer):
    i = pl.program_id(1)
    conv_halo = SUBLANES * CONV_BACK
    pool_halo = SUBLANES * POOL_BACK

    @pl.when(i == 0)
    def _start_of_sequence():
        atail[...] = jnp.zeros(atail.shape, F32)
        ptail[...] = jnp.zeros(ptail.shape, F32)
        memb = mem_ref[0].astype(BF16)
        kt = lax.dot_general(w_mkt_ref[...], memb, (((1,), (1,)), ((), ())),
                             preferred_element_type=F32) * (MEM_HEAD_DIM ** -0.5)
        v = jnp.dot(memb, w_mv_ref[...], preferred_element_type=F32)
        kshape = (D_MEM, N_MEM_HEADS * MEM_LEN)
        k_row_head = lax.broadcasted_iota(jnp.int32, kshape, 0) // MEM_HEAD_DIM
        k_col_head = lax.broadcasted_iota(jnp.int32, kshape, 1) // MEM_LEN
        kt4 = jnp.concatenate([kt] * N_MEM_HEADS, axis=1)
        kbd[...] = jnp.where(k_row_head == k_col_head, kt4, 0.0).astype(BF16)
        vshape = (N_MEM_HEADS * MEM_LEN, D_MEM)
        v_row_head = lax.broadcasted_iota(jnp.int32, vshape, 0) // MEM_LEN
        v_col_head = lax.broadcasted_iota(jnp.int32, vshape, 1) // MEM_HEAD_DIM
        v4 = jnp.concatenate([v] * N_MEM_HEADS, axis=0)
        same_head = v_row_head == v_col_head
        vbd[:, 0:D_MEM] = jnp.where(same_head, v4, 0.0).astype(BF16)
        vbd[:, D_MEM:2 * D_MEM] = jnp.where(same_head, 1.0, 0.0).astype(BF16)

    st = [dict() for _ in range(TILES_PER_STEP)]

    def rows_of(t):
        return slice(t * tq, (t + 1) * tq)

    def front_x(t):
        x = x_ref[0, rows_of(t), :]
        if first_layer:
            x = _to_segment_major(_layer_norm(x, ln0_g_ref[...], ln0_b_ref[...]), tq)
        st[t].update(x=x, xb=x.astype(BF16))

    def front_a(t, l):
        lanes = slice(LANES * l, LANES * (l + 1))
        h = jnp.dot(st[t]["xb"], w_in_ref[:, 2 * LANES * l:2 * LANES * (l + 1)], preferred_element_type=F32)
        a = h[:, 0:LANES] * jax.nn.sigmoid(h[:, LANES:2 * LANES])
        a_prev = atail[:, lanes] if t == 0 else st[t - 1]["a_tail"][l]
        a_tail = a[tq - conv_halo:tq, :]
        ebuf[t, 0:conv_halo, lanes] = _halo(a_tail, a_prev, amix.at[t, l])
        ebuf[t, conv_halo:conv_halo + tq, lanes] = a
        st[t].setdefault("a_tail", {})[l] = a_tail

    def front_p(t):
        p_in = jnp.dot(st[t]["xb"], w_in_ref[:, 2 * D_CONV:2 * D_CONV + D_POOL], preferred_element_type=F32)
        p_prev = ptail[...] if t == 0 else st[t - 1]["p_tail"]
        p_tail = p_in[tq - pool_halo:tq, :]
        pbuf[t, 0:pool_halo, :] = _halo(p_tail, p_prev, pmix.at[t])
        pbuf[t, pool_halo:pool_halo + tq, :] = p_in
        st[t].update(p_tail=p_tail)

    def front_q(t):
        q = jnp.dot(st[t]["xb"], w_in_ref[:, 2 * D_CONV + D_POOL:D_IN], preferred_element_type=F32)
        qbuf[t] = q.astype(BF16)

    def conv(t, l, r0):
        lanes = slice(LANES * l, LANES * (l + 1))
        accs = [jnp.zeros((SUBLANES, LANES), F32)] * (ROW_CHUNK // SUBLANES)
        for j in range(CONV_K):
            wj = conv_w_ref[SUBLANES * j:SUBLANES * (j + 1), lanes]
            for k in range(len(accs)):
                lo = r0 + SUBLANES * (j + k)
                accs[k] = accs[k] + wj * ebuf[t, lo:lo + SUBLANES, lanes]
        cbuf[t, r0:r0 + ROW_CHUNK, lanes] = jnp.concatenate(accs, axis=0) + conv_b_ref[:, lanes]

    def conv_norm(t):
        y = _layer_norm(cbuf[t], cln_g_ref[...], cln_b_ref[...])
        y = y * jax.nn.sigmoid(y)
        cat[t, :, 0:D_CONV] = y.astype(BF16)

    def pool(t):
        ext = pbuf[t]
        p_in = ext[pool_halo:, :]
        wins = []
        for lane0 in range(0, D_POOL, LANES):
            lo_w = POOL_WINDOWS[lane0 // POOL_GROUP]
            hi_w = POOL_WINDOWS[(lane0 + LANES - 1) // POOL_GROUP]
            split = (lane0 // POOL_GROUP + 1) * POOL_GROUP - lane0
            s = ext[:, lane0:lane0 + LANES]
            sums = {}
            span = 1
            while span < hi_w:
                shift = SUBLANES * span
                s = s[shift:, :] + s[:-shift, :]
                span *= 2
                if span in (lo_w, hi_w):
                    sums[span] = s[s.shape[0] - tq:, :]
            lane = lax.broadcasted_iota(jnp.int32, (tq, LANES), 1)
            wins.append(jnp.where(lane < split, sums[lo_w], sums[hi_w]))
        win = jnp.concatenate(wins, axis=1)
        count = count_first_ref[0] if t == 0 else count_later_ref[0]
        d = (win / count - p_in).astype(BF16)
        pooled = jnp.dot(d, w_pool_ref[...], preferred_element_type=F32) * pool_scale_ref[...]
        cat[t, :, D_CONV:D_CONV + D_POOL] = pooled.astype(BF16)

    def attn(t):
        s = jnp.dot(qbuf[t], kbd[...], preferred_element_type=F32)
        es = []
        for hd in range(N_MEM_HEADS):
            sh = s[:, hd * MEM_LEN:(hd + 1) * MEM_LEN]
            es.append(jnp.exp(sh - jnp.max(sh, axis=-1, keepdims=True)))
        e = jnp.concatenate(es, axis=1).astype(BF16)
        ov = jnp.dot(e, vbd[...], preferred_element_type=F32)
        cat[t, :, D_CONV + D_POOL:] = (ov[:, 0:D_MEM] / ov[:, D_MEM:2 * D_MEM]).astype(BF16)

    def out(t):
        y = jnp.dot(cat[t], w_out_ref[...], preferred_element_type=F32)
        o_ref[0, rows_of(t), :] = _layer_norm(DEEPNORM_ALPHA * st[t]["x"] + y, ln_g_ref[...], ln_b_ref[...])

    n_lane = D_CONV // LANES

    def convs(t, l):
        for r0 in range(0, tq, ROW_CHUNK):
            conv(t, l, r0)

    front_x(0); front_a(0, 0); front_a(0, 1)
    convs(0, 0); front_a(0, 2)
    convs(0, 1); front_p(0); front_q(0); front_x(1)
    convs(0, 2); front_a(1, 0); front_a(1, 1)
    conv_norm(0); pool(0)
    convs(1, 0); attn(0); front_a(1, 2)
    convs(1, 1); out(0); front_p(1); front_q(1)
    convs(1, 2)
    conv_norm(1); pool(1); attn(1); out(1)

    last = st[TILES_PER_STEP - 1]
    for l in range(n_lane):
        atail[:, LANES * l:LANES * (l + 1)] = last["a_tail"][l]
    ptail[...] = last["p_tail"]


def _ff_chunks():
    chunks = []
    c0 = 0
    while c0 < D_FF:
        cn = min(FF_CHUNK, D_FF - c0)
        chunks.append((c0, cn))
        c0 += cn
    return chunks


def _ffn_kernel(x_ref, w_up_ref, conv_w_ref, conv_b_ref, w_down_ref, ln_g_ref, ln_b_ref,
                o_ref, utail, umix, act, *, tq, last_layer):
    i = pl.program_id(1)
    halo = SUBLANES * FFN_BACK

    @pl.when(i == 0)
    def _start_of_sequence():
        utail[...] = jnp.zeros(utail.shape, F32)

    u_prev = {}
    xs = {}
    xbs = {}

    def prep(t):
        x = x_ref[0, t * tq:(t + 1) * tq, :]
        xs[t] = x
        xbs[t] = x.astype(BF16)

    def conv_part(t, col, cn):
        cols = slice(col, col + cn)
        u = jnp.dot(xbs[t], w_up_ref[:, cols], preferred_element_type=F32)
        u_tail = u[tq - halo:tq, :]
        prev = utail[:, cols] if t == 0 else u_prev[col]
        ext = jnp.concatenate([_halo(u_tail, prev, umix.at[t, col // FF_CHUNK]), u], axis=0)
        u_prev[col] = u_tail
        y = conv_b_ref[:, cols] + conv_w_ref[FFN_BACK:FFN_K, cols] * u
        for k in range(FFN_BACK):
            y = y + conv_w_ref[k:k + 1, cols] * ext[SUBLANES * k:SUBLANES * k + tq, :]
        return y

    def chunk(t, c0, cn):
        g = conv_part(t, c0, cn)
        v = conv_part(t, D_FF + c0, cn)
        act[t, :, c0:c0 + cn] = (g * jax.nn.sigmoid(g) * v).astype(BF16)

    def down(t):
        rows = slice(t * tq, (t + 1) * tq)
        y = jnp.dot(act[t], w_down_ref[...], preferred_element_type=F32)
        out = _layer_norm(DEEPNORM_ALPHA * xs[t] + y, ln_g_ref[...], ln_b_ref[...])
        o_ref[0, rows, :] = _to_frame_order(out, tq) if last_layer else out

    for t in range(TILES_PER_STEP):
        prep(t)
    for c0, cn in _ff_chunks():
        for t in range(TILES_PER_STEP):
            chunk(t, c0, cn)
    for t in range(TILES_PER_STEP):
        down(t)

    for col, u_tail in u_prev.items():
        utail[:, col:col + u_tail.shape[1]] = u_tail


def _whole(shape):
    return pl.BlockSpec(shape, lambda b, i: (0,) * len(shape), pipeline_mode=pl.Buffered(1))


def _row(v):
    return v.reshape(1, -1)


def _params():
    return pltpu.CompilerParams(dimension_semantics=("arbitrary", "arbitrary"),
                                vmem_limit_bytes=VMEM_LIMIT_BYTES)


def _pool_counts(tq):
    rows = jnp.arange(tq)
    frame = (rows % SUBLANES) * (tq // SUBLANES) + rows // SUBLANES
    window = jnp.repeat(jnp.asarray(POOL_WINDOWS, jnp.int32), POOL_GROUP)
    first = jnp.minimum(frame[:, None] + 1, window[None, :])
    later = jnp.broadcast_to(window[None, :], (tq, D_POOL))
    return jnp.stack([first, later]).astype(F32)


def _mixer_call(x, mem, ln0_g, ln0_b, w_in, conv_w, conv_b, cln_g, cln_b, w_pool_bd, pool_scale, w_mkt, w_mv, w_out,
                ln_g, ln_b, tq, first_layer):
    bsz, seq, d = x.shape
    step_rows = TILES_PER_STEP * tq
    tile = pl.BlockSpec((1, step_rows, d), lambda b_, i: (b_, i, 0))
    mem_spec = pl.BlockSpec((1, MEM_LEN, d), lambda b_, i: (b_, 0, 0))
    count_first = pl.BlockSpec((1, tq, D_POOL), lambda b_, i: (jnp.minimum(i, 1), 0, 0))
    count_later = pl.BlockSpec((1, tq, D_POOL), lambda b_, i: (1, 0, 0), pipeline_mode=pl.Buffered(1))
    counts = _pool_counts(tq)
    conv_w_rows = jnp.repeat(conv_w, SUBLANES, axis=0)
    args = [x, mem, _row(ln0_g), _row(ln0_b), w_in, conv_w_rows, _row(conv_b), _row(cln_g), _row(cln_b), w_pool_bd,
            _row(pool_scale), counts, counts, w_mkt, w_mv, w_out, _row(ln_g), _row(ln_b)]
    in_specs = [tile, mem_spec] + [_whole(a.shape) for a in args[2:]]
    in_specs[11], in_specs[12] = count_first, count_later
    return pl.pallas_call(
        functools.partial(_mixer_kernel, tq=tq, first_layer=first_layer),
        grid=(bsz, seq // step_rows),
        in_specs=in_specs,
        out_specs=tile,
        out_shape=jax.ShapeDtypeStruct(x.shape, x.dtype),
        scratch_shapes=[
            pltpu.VMEM((TILES_PER_STEP, tq, D_CONV), F32),
            pltpu.VMEM((TILES_PER_STEP, SUBLANES * CONV_BACK + tq, D_CONV), F32),
            pltpu.VMEM((TILES_PER_STEP, SUBLANES * POOL_BACK + tq, D_POOL), F32),
            pltpu.VMEM((TILES_PER_STEP, tq, D_MEM), BF16),
            pltpu.VMEM((SUBLANES * CONV_BACK, D_CONV), F32),
            pltpu.VMEM((SUBLANES * POOL_BACK, D_POOL), F32),
            pltpu.VMEM((TILES_PER_STEP, D_CONV // LANES, SUBLANES * (CONV_BACK + 1), LANES), F32),
            pltpu.VMEM((TILES_PER_STEP, SUBLANES * (POOL_BACK + 1), D_POOL), F32),
            pltpu.VMEM((D_MEM, N_MEM_HEADS * MEM_LEN), BF16),
            pltpu.VMEM((N_MEM_HEADS * MEM_LEN, 2 * D_MEM), BF16),
            pltpu.VMEM((TILES_PER_STEP, tq, D_MODEL), BF16),
        ],
        compiler_params=_params(),
        name="mixer",
    )(*args)


def _ffn_call(x, w_up, conv_w, conv_b, w_down, ln_g, ln_b, tq, last_layer):
    bsz, seq, d = x.shape
    step_rows = TILES_PER_STEP * tq
    tile = pl.BlockSpec((1, step_rows, d), lambda b_, i: (b_, i, 0))
    args = (x, w_up, conv_w, _row(conv_b), w_down, _row(ln_g), _row(ln_b))
    in_specs = [tile] + [_whole(a.shape) for a in args[1:]]
    return pl.pallas_call(
        functools.partial(_ffn_kernel, tq=tq, last_layer=last_layer),
        grid=(bsz, seq // step_rows),
        in_specs=in_specs,
        out_specs=tile,
        out_shape=jax.ShapeDtypeStruct(x.shape, x.dtype),
        scratch_shapes=[
            pltpu.VMEM((SUBLANES * FFN_BACK, 2 * D_FF), F32),
            pltpu.VMEM((TILES_PER_STEP, 2 * D_FF // FF_CHUNK, SUBLANES * (FFN_BACK + 1), FF_CHUNK), F32),
            pltpu.VMEM((TILES_PER_STEP, tq, D_FF), BF16),
        ],
        compiler_params=_params(),
        name="ffn",
    )(*args)


def _pair_glu_columns(w_in):
    pieces = []
    for l in range(D_CONV // LANES):
        pieces += [w_in[:, LANES * l:LANES * (l + 1)], w_in[:, D_CONV + LANES * l:D_CONV + LANES * (l + 1)]]
    return jnp.concatenate(pieces + [w_in[:, 2 * D_CONV:]], axis=1)


def _block_diag(w):
    g, n, _ = w.shape
    eye = jnp.eye(g, dtype=w.dtype)
    return jnp.einsum("gh,gcd->gchd", eye, w).reshape(g * n, g * n)


def kernel(x, mem, ln0_g, ln0_b, w_in, conv_w, conv_b, conv_ln_g, conv_ln_b, pool_w, pool_scale, w_mk, w_mv, w_out, ln1_g, ln1_b, w_up, ffn_conv_w, ffn_conv_b, w_down, ln2_g, ln2_b):
    bsz, seq, d = x.shape
    assert d == D_MODEL and mem.shape == (bsz, MEM_LEN, D_MODEL)
    tq = ROW_TILE
    assert TILES_PER_STEP == 2, "the mixer's program order is written out for two tiles per step"
    assert D_FF % FF_CHUNK == 0, "halo staging is indexed by whole chunks"
    assert seq % (TILES_PER_STEP * tq) == 0 and tq % ROW_CHUNK == 0 and tq // SUBLANES > CONV_BACK
    for l in range(DEPTH):
        x = _mixer_call(
            x, mem, ln0_g, ln0_b, _pair_glu_columns(w_in[l]).astype(BF16), conv_w[l], conv_b[l], conv_ln_g[l],
            conv_ln_b[l],
            _block_diag(pool_w[l]).astype(BF16), pool_scale[l], w_mk[l].T.astype(BF16), w_mv[l].astype(BF16),
            w_out[l].astype(BF16), ln1_g[l], ln1_b[l], tq, first_layer=(l == 0))
        x = _ffn_call(x, w_up[l].astype(BF16), ffn_conv_w[l], ffn_conv_b[l], w_down[l].astype(BF16),
                      ln2_g[l], ln2_b[l], tq, last_layer=(l == DEPTH - 1))
    return x
```
